```python
import math
import jax
import jax.numpy as jnp
from jax import lax
import numpy as np

D_MODEL = 2048
BATCH = 8
SEQ = 2048
DEPTH = 4

ATT_HEADS = 8
KV_RANK = 256
IDX_HEADS = 8
IDX_DIM = 64
TOPK_MAX = 256
Q_BLOCK = 128
REL_BUCKETS = 32
REL_MAX_DIST = 128
M_HEADS = 4
M_QK_DIM = 128
M_V_DIM = 256
M_CHUNK = 64
CONV_WIDTH = 4
D_FF = 5504
EPS = 1e-6

COL_WIDTHS = (
    ATT_HEADS * KV_RANK,
    KV_RANK,
    IDX_HEADS * IDX_DIM,
    IDX_DIM,
    IDX_HEADS,
    2 * M_HEADS * M_QK_DIM,
    M_HEADS * M_V_DIM,
    M_HEADS * M_V_DIM,
    M_HEADS,
    M_HEADS,
    D_MODEL,
    D_MODEL,
)
D_IN = sum(COL_WIDTHS)
SPLIT_POINTS = tuple(int(v) for v in np.cumsum(COL_WIDTHS)[:-1])

kernel_name = 'hybrid_dsa_mlstm_macaron_trunk'


def rms_norm(x, g):
    xf = x.astype(jnp.float32)
    y = xf * lax.rsqrt(jnp.mean(xf * xf, axis=-1, keepdims=True) + EPS)
    return (y * g.astype(jnp.float32)).astype(x.dtype)


def swiglu(x, w_gu, w_down):
    gate, up = jnp.split(x @ w_gu, 2, axis=-1)
    return (jax.nn.silu(gate) * up) @ w_down


def t5_bucket(rel):
    n = jnp.maximum(rel, 0)
    max_exact = REL_BUCKETS // 2
    n_large = jnp.maximum(n, max_exact).astype(jnp.float32)
    large = max_exact + (jnp.log(n_large / max_exact) / math.log(REL_MAX_DIST / max_exact)
                         * (REL_BUCKETS - max_exact)).astype(jnp.int32)
    large = jnp.minimum(large, REL_BUCKETS - 1)
    return jnp.where(n < max_exact, n, large)


def causal_conv(x, w, b):
    c = x.shape[-1]
    y = lax.conv_general_dilated(x, w[:, None, :], window_strides=(1,),
                                 padding=((CONV_WIDTH - 1, 0),),
                                 dimension_numbers=('NWC', 'WIO', 'NWC'),
                                 feature_group_count=c)
    return y + b


def dsa_attention(q, c_kv, q_idx, k_idx, w_idx, rel_bias, topk):
    b, s = q.shape[:2]
    n_blocks = s // Q_BLOCK
    scale = KV_RANK ** -0.5
    s_pos = jnp.arange(s)

    def block(t0):
        qb = lax.dynamic_slice_in_dim(q, t0, Q_BLOCK, axis=1)
        qib = lax.dynamic_slice_in_dim(q_idx, t0, Q_BLOCK, axis=1)
        wib = lax.dynamic_slice_in_dim(w_idx, t0, Q_BLOCK, axis=1)
        t_pos = t0 + jnp.arange(Q_BLOCK)
        causal = s_pos[None, :] <= t_pos[:, None]
        dots = jnp.einsum('bthd,bsd->bths', qib, k_idx).astype(jnp.float32)
        score = jnp.einsum('bths,bth->bts', jax.nn.relu(dots), wib.astype(jnp.float32))
        score = jnp.where(causal[None], score, -jnp.inf)
        _, idx = lax.top_k(score, topk)
        valid = idx <= t_pos[None, :, None]
        kv_sel = jax.vmap(lambda c, i: c[i])(c_kv, idx)
        logits = jnp.einsum('bthr,btkr->bthk', qb, kv_sel).astype(jnp.float32) * scale
        bias = rel_bias[t5_bucket(t_pos[None, :, None] - idx)]
        logits = logits + jnp.transpose(bias, (0, 1, 3, 2)).astype(jnp.float32)
        logits = jnp.where(valid[:, :, None, :], logits, -jnp.inf)
        p = jax.nn.softmax(logits, axis=-1).astype(q.dtype)
        return jnp.einsum('bthk,btkr->bthr', p, kv_sel)

    out = lax.map(block, jnp.arange(n_blocks) * Q_BLOCK)
    return jnp.transpose(out, (1, 0, 2, 3, 4)).reshape(b, s, ATT_HEADS, KV_RANK)


def mlstm(q, k, v, i_pre, f_pre):
    out_dtype = v.dtype
    b, s = q.shape[:2]
    f32 = jnp.float32
    q = q.astype(f32)
    k = k.astype(f32) * (M_QK_DIM ** -0.5)
    v = v.astype(f32)
    ig = i_pre.astype(f32)
    lf = jax.nn.log_sigmoid(f_pre.astype(f32))

    def chunks(a):
        a = a.reshape((b, s // M_CHUNK, M_CHUNK) + a.shape[2:])
        return jnp.transpose(a, (1, 0, 3, 2) + tuple(range(4, a.ndim)))

    tril = jnp.tril(jnp.ones((M_CHUNK, M_CHUNK), dtype=bool))

    def step(carry, inp):
        c_st, n_st, m_st = carry
        qc, kc, vc, ic, fc = inp
        bcum = jnp.cumsum(fc, axis=-1)
        dmat = jnp.where(tril, bcum[..., :, None] - bcum[..., None, :] + ic[..., None, :], -jnp.inf)
        inter = bcum + m_st[..., None]
        m_row = jnp.maximum(inter, jnp.max(dmat, axis=-1))
        w_inter = jnp.exp(inter - m_row)
        qk = jnp.einsum('bhld,bhsd->bhls', qc, kc) * jnp.exp(dmat - m_row[..., None])
        num = (w_inter[..., None] * jnp.einsum('bhld,bhdv->bhlv', qc, c_st)
               + jnp.einsum('bhls,bhsv->bhlv', qk, vc))
        den = w_inter * jnp.einsum('bhld,bhd->bhl', qc, n_st) + jnp.sum(qk, axis=-1)
        h = num / jnp.maximum(jnp.abs(den), jnp.exp(-m_row))[..., None]
        b_last = bcum[..., -1]
        g = b_last[..., None] - bcum + ic
        m_new = jnp.maximum(b_last + m_st, jnp.max(g, axis=-1))
        decay = jnp.exp(b_last + m_st - m_new)
        wk = jnp.exp(g - m_new[..., None])
        c_new = decay[..., None, None] * c_st + jnp.einsum('bhs,bhsd,bhsv->bhdv', wk, kc, vc)
        n_new = decay[..., None] * n_st + jnp.einsum('bhs,bhsd->bhd', wk, kc)
        return (c_new, n_new, m_new), h

    init = (jnp.zeros((b, M_HEADS, M_QK_DIM, M_V_DIM), f32),
            jnp.zeros((b, M_HEADS, M_QK_DIM), f32),
            jnp.zeros((b, M_HEADS), f32))
    _, h = lax.scan(step, init, (chunks(q), chunks(k), chunks(v), chunks(ig), chunks(lf)))
    h = jnp.transpose(h, (1, 0, 3, 2, 4)).reshape(b, s, M_HEADS, M_V_DIM)
    return h.astype(out_dtype)


def setup_inputs(seed: int = 0) -> dict:
    key = jax.random.key(seed)
    ks = jax.random.split(key, 20)
    f32 = jnp.float32

    def nrm(k, shape, scale):
        return jax.random.normal(k, shape, f32) * scale

    def gain(k, shape):
        return 1.0 + 0.05 * jax.random.normal(k, shape, f32)

    L, D, F = DEPTH, D_MODEL, D_FF
    CQK = 2 * M_HEADS * M_QK_DIM
    return {
        'x': nrm(ks[0], (BATCH, SEQ, D), 1.0),
        'rel_bias': nrm(ks[1], (REL_BUCKETS, ATT_HEADS), 0.5),
        'ffn1_norm': gain(ks[2], (L, D)),
        'ffn1_w_gu': nrm(ks[3], (L, D, 2 * F), D ** -0.5),
        'ffn1_w_down': nrm(ks[4], (L, F, D), F ** -0.5),
        'mix_norm': gain(ks[5], (L, D)),
        'w_in': nrm(ks[6], (L, D, D_IN), D ** -0.5),
        'q_norm': gain(ks[7], (L, KV_RANK)),
        'kv_norm': gain(ks[8], (L, KV_RANK)),
        'conv_w': nrm(ks[9], (L, CONV_WIDTH, CQK), CONV_WIDTH ** -0.5),
        'conv_b': nrm(ks[10], (L, CQK), 0.02),
        'igate_b': nrm(ks[11], (L, M_HEADS), 0.1),
        'fgate_b': jnp.linspace(3.0, 6.0, M_HEADS, dtype=f32)[None, :] + nrm(ks[12], (L, M_HEADS), 0.1),
        'm_out_norm': gain(ks[13], (L, M_HEADS, M_V_DIM)),
        'w_att_out': nrm(ks[14], (L, ATT_HEADS, KV_RANK, D), (ATT_HEADS * KV_RANK) ** -0.5),
        'w_mem_out': nrm(ks[15], (L, M_HEADS * M_V_DIM, D), (M_HEADS * M_V_DIM) ** -0.5),
        'w_out': nrm(ks[16], (L, D, D), D ** -0.5),
        'ffn2_norm': gain(ks[17], (L, D)),
        'ffn2_w_gu': nrm(ks[18], (L, D, 2 * F), D ** -0.5),
        'ffn2_w_down': nrm(ks[19], (L, F, D), F ** -0.5),
    }


def reference(x, rel_bias, ffn1_norm, ffn1_w_gu, ffn1_w_down, mix_norm, w_in, q_norm, kv_norm,
              conv_w, conv_b, igate_b, fgate_b, m_out_norm, w_att_out, w_mem_out, w_out,
              ffn2_norm, ffn2_w_gu, ffn2_w_down):
    b, s, _ = x.shape
    topk = min(TOPK_MAX, s // 4)
    h = x
    for l in range(DEPTH):
        h = h + 0.5 * swiglu(rms_norm(h, ffn1_norm[l]), ffn1_w_gu[l], ffn1_w_down[l])
        u = rms_norm(h, mix_norm[l])
        proj = u @ w_in[l]
        (aq, ckv, iq, ik, iw, mqk, mv, mo, mi, mf, ga, gm) = jnp.split(proj, SPLIT_POINTS, axis=-1)
        aq = rms_norm(aq.reshape(b, s, ATT_HEADS, KV_RANK), q_norm[l])
        ckv = rms_norm(ckv, kv_norm[l])
        iq = iq.reshape(b, s, IDX_HEADS, IDX_DIM)
        iw = iw * (IDX_HEADS * IDX_DIM) ** -0.5
        o_att = dsa_attention(aq, ckv, iq, ik, iw, rel_bias, topk)
        y_att = jnp.einsum('bshr,hrd->bsd', o_att, w_att_out[l])
        mqk = jax.nn.silu(causal_conv(mqk, conv_w[l], conv_b[l]))
        mq, mk = jnp.split(mqk, 2, axis=-1)
        hm = mlstm(mq.reshape(b, s, M_HEADS, M_QK_DIM), mk.reshape(b, s, M_HEADS, M_QK_DIM),
                   mv.reshape(b, s, M_HEADS, M_V_DIM), mi + igate_b[l], mf + fgate_b[l])
        hm = rms_norm(hm, m_out_norm[l]).reshape(b, s, M_HEADS * M_V_DIM) * jax.nn.sigmoid(mo)
        y_mem = hm @ w_mem_out[l]
        merged = jax.nn.sigmoid(ga) * y_att + jax.nn.sigmoid(gm) * y_mem
        h = h + merged @ w_out[l]
        h = h + 0.5 * swiglu(rms_norm(h, ffn2_norm[l]), ffn2_w_gu[l], ffn2_w_down[l])
    return h
```

```python
import functools
import math

import jax
import jax.numpy as jnp
from jax import lax
from jax.experimental import pallas as pl
from jax.experimental.pallas import tpu as pltpu

F32 = jnp.float32
BF16 = jnp.bfloat16

ATT_HEADS = 8
KV_RANK = 256
IDX_HEADS = 8
IDX_DIM = 64
TOPK_MAX = 256
Q_BLOCK = 128
REL_BUCKETS = 32
REL_MAX_DIST = 128
M_HEADS = 4
M_QK_DIM = 128
M_V_DIM = 256
M_CHUNK = 64
CONV_WIDTH = 4
EPS = 1e-6

HR = ATT_HEADS * KV_RANK
IQW = IDX_HEADS * IDX_DIM
MQK = 2 * M_HEADS * M_QK_DIM
MV = M_HEADS * M_V_DIM
LANE = 128
VMEM_LIMIT = 56 * 1024 * 1024

OFF_AQ = 0
OFF_IQ = OFF_AQ + HR
OFF_CKV = OFF_IQ + IQW
OFF_IKW = OFF_CKV + KV_RANK
OFF_MG = OFF_IKW + LANE
OFF_MQK = OFF_MG + LANE
OFF_MV = OFF_MQK + MQK
OFF_MO = OFF_MV + MV
OFF_GA = OFF_MO + MV

KEY_NEG_INF = -2139095041


def _params(n_grid):
    return pltpu.CompilerParams(dimension_semantics=("arbitrary",) * n_grid,
                                vmem_limit_bytes=VMEM_LIMIT)


def _tile(n, pref):
    if n <= pref:
        return n
    t = (pref // LANE) * LANE
    while n % t:
        t -= LANE
    return t


def _rms(x, g):
    return x * lax.rsqrt(jnp.mean(x * x, axis=-1, keepdims=True) + EPS) * g


def _sigmoid(x):
    return 1.0 / (1.0 + jnp.exp(-x))


def _log_sigmoid(x):
    return jnp.minimum(x, 0.0) - jnp.log(1.0 + jnp.exp(-jnp.abs(x)))


def _ffn_kernel(h_ref, g_ref, wg_ref, wu_ref, wd_ref, o_ref, xn_ref):
    j = pl.program_id(1)

    @pl.when(j == 0)
    def _():
        x = h_ref[...]
        xn_ref[...] = _rms(x, g_ref[...]).astype(BF16)
        o_ref[...] = x

    xn = xn_ref[...]
    g = jnp.dot(xn, wg_ref[...], preferred_element_type=F32)
    u = jnp.dot(xn, wu_ref[...], preferred_element_type=F32)
    a = (g * _sigmoid(g) * u * 0.5).astype(BF16)
    o_ref[...] += jnp.dot(a, wd_ref[...], preferred_element_type=F32)


def _ffn(h, gain, wg, wu, wd):
    m, d = h.shape
    fp = wg.shape[1]
    tm = _tile(m, 512)
    tf = _tile(fp, 512)
    return pl.pallas_call(
        _ffn_kernel,
        grid=(m // tm, fp // tf),
        in_specs=[
            pl.BlockSpec((tm, d), lambda i, j: (i, 0)),
            pl.BlockSpec((1, d), lambda i, j: (0, 0)),
            pl.BlockSpec((d, tf), lambda i, j: (0, j)),
            pl.BlockSpec((d, tf), lambda i, j: (0, j)),
            pl.BlockSpec((tf, d), lambda i, j: (j, 0)),
        ],
        out_specs=pl.BlockSpec((tm, d), lambda i, j: (i, 0)),
        out_shape=jax.ShapeDtypeStruct((m, d), F32),
        scratch_shapes=[pltpu.VMEM((tm, d), BF16)],
        compiler_params=_params(2),
        name="ffn",
    )(h, gain, wg, wu, wd)


def _inproj_kernel(h_ref, g_ref, w_ref, o_ref, xn_ref):
    @pl.when(pl.program_id(1) == 0)
    def _():
        xn_ref[...] = _rms(h_ref[...], g_ref[...]).astype(BF16)

    o_ref[...] = jnp.dot(xn_ref[...], w_ref[...], preferred_element_type=F32)


def _inproj(h, gain, w):
    m, d = h.shape
    n = w.shape[1]
    tm = _tile(m, 1024)
    tn = _tile(n, 512)
    return pl.pallas_call(
        _inproj_kernel,
        grid=(m // tm, n // tn),
        in_specs=[
            pl.BlockSpec((tm, d), lambda i, j: (i, 0)),
            pl.BlockSpec((1, d), lambda i, j: (0, 0)),
            pl.BlockSpec((d, tn), lambda i, j: (0, j)),
        ],
        out_specs=pl.BlockSpec((tm, tn), lambda i, j: (i, j)),
        out_shape=jax.ShapeDtypeStruct((m, n), F32),
        scratch_shapes=[pltpu.VMEM((tm, d), BF16)],
        compiler_params=_params(2),
        name="inproj",
    )(h, gain, w)


def _dsa_kernel(aq_ref, iq_ref, ckv_ref, ik_ref, iw_ref, bias_ref, qn_ref, kn_ref, o_ref,
                ckv_s, ik_s, q_s, iq_s, wb_s, sc_s, lg_s, m_s, l_s, acc_s, *, topk):
    qb = pl.program_id(1)
    tq = Q_BLOCK
    nh = ATT_HEADS

    @pl.when(qb == 0)
    def _():
        ckv_s[...] = _rms(ckv_ref[...], kn_ref[...]).astype(BF16)
        ik_s[...] = ik_ref[:, :IDX_DIM].astype(BF16)

    scale = KV_RANK ** -0.5
    for h in range(nh):
        qh = _rms(aq_ref[:, h * KV_RANK:(h + 1) * KV_RANK], qn_ref[...]) * scale
        q_s[h * tq:(h + 1) * tq, :] = qh.astype(BF16)
    for h in range(IDX_HEADS):
        iq_s[h * tq:(h + 1) * tq, :] = iq_ref[:, h * IDX_DIM:(h + 1) * IDX_DIM].astype(BF16)
    w = iw_ref[:, IDX_DIM:IDX_DIM + IDX_HEADS] * (IQW ** -0.5)
    for h in range(IDX_HEADS):
        wb_s[h] = jnp.broadcast_to(w[:, h:h + 1], (tq, tq))

    row = lax.broadcasted_iota(jnp.int32, (tq, tq), 0)
    col = lax.broadcasted_iota(jnp.int32, (tq, tq), 1)
    nt_dims = (((1,), (1,)), ((), ()))

    def causal_mask(kb):
        return (kb * tq + col) <= (qb * tq + row)

    def score_body(kb, c):
        off = pl.multiple_of(kb * tq, tq)
        d = lax.dot_general(iq_s[...], ik_s[pl.ds(off, tq), :], nt_dims, preferred_element_type=F32)
        d = jnp.maximum(d, 0.0).reshape(IDX_HEADS, tq, tq)
        s = jnp.sum(d * wb_s[...], axis=0)
        sc_s[kb] = jnp.where(causal_mask(kb), s, -jnp.inf)
        return c

    lax.fori_loop(0, qb + 1, score_body, 0)

    ngrp = (qb + 4) // 4

    def fill_body(kb, c):
        sc_s[kb] = jnp.full((tq, tq), -jnp.inf, F32)
        return c

    lax.fori_loop(qb + 1, ngrp * 4, fill_body, 0)

    kf = float(topk)

    def bit_body(it, t_key):
        cand = t_key + lax.shift_left(jnp.int32(1), 31 - it)
        cbits = jnp.where(cand >= 0, cand, cand ^ jnp.int32(0x7FFFFFFF))
        cf = jnp.broadcast_to(lax.bitcast_convert_type(cbits, F32), (tq, tq))

        def grp_body(g, acc):
            for u in range(4):
                acc = acc + jnp.where(sc_s[g * 4 + u] >= cf, 1.0, 0.0)
            return acc

        acc = lax.fori_loop(0, ngrp, grp_body, jnp.zeros((tq, tq), F32))
        cnt = jnp.sum(acc, axis=-1, keepdims=True)
        ok = (cnt >= kf) | (cand < KEY_NEG_INF)
        return jnp.where(ok, cand, t_key)

    t_key = lax.fori_loop(0, 32, bit_body, jnp.full((tq, 1), -2 ** 31, jnp.int32))
    t_bits = jnp.where(t_key >= 0, t_key, t_key ^ jnp.int32(0x7FFFFFFF))
    thr = jnp.broadcast_to(lax.bitcast_convert_type(t_bits, F32), (tq, tq))

    m_s[...] = jnp.full(m_s.shape, -jnp.inf, F32)

    def logit_body(kb, c):
        off = pl.multiple_of(kb * tq, tq)
        lg = lax.dot_general(q_s[...], ckv_s[pl.ds(off, tq), :], nt_dims, preferred_element_type=F32)
        lg = lg.reshape(nh, tq, tq) + bias_ref[jnp.minimum(qb - kb, 2)]
        keep = (sc_s[kb] >= thr) & causal_mask(kb)
        lg = jnp.where(keep[None], lg, -jnp.inf)
        lg_s[kb] = lg
        m_s[...] = jnp.maximum(m_s[...], lg)
        return c

    lax.fori_loop(0, qb + 1, logit_body, 0)

    m = jnp.max(m_s[...], axis=-1, keepdims=True)
    m_s[...] = jnp.broadcast_to(m, m_s.shape)
    l_s[...] = jnp.zeros(l_s.shape, F32)
    acc_s[...] = jnp.zeros(acc_s.shape, F32)

    def pv_body(kb, c):
        off = pl.multiple_of(kb * tq, tq)
        p = jnp.exp(lg_s[kb] - m_s[...])
        l_s[...] += p
        acc_s[...] += jnp.dot(p.reshape(nh * tq, tq).astype(BF16), ckv_s[pl.ds(off, tq), :],
                              preferred_element_type=F32)
        return c

    lax.fori_loop(0, qb + 1, pv_body, 0)

    lsum = jnp.sum(l_s[...], axis=-1, keepdims=True)
    for h in range(nh):
        o_ref[:, h * KV_RANK:(h + 1) * KV_RANK] = (acc_s[h * tq:(h + 1) * tq, :] / lsum[h]).astype(o_ref.dtype)


def _dsa(p3, bias_tiles, q_gain, kv_gain, topk):
    b, s, _ = p3.shape
    tq = Q_BLOCK
    nkb = s // tq
    assert nkb % 4 == 0
    kern = functools.partial(_dsa_kernel, topk=topk)
    return pl.pallas_call(
        kern,
        grid=(b, nkb),
        in_specs=[
            pl.BlockSpec((None, tq, HR), lambda i, j: (i, j, OFF_AQ // HR)),
            pl.BlockSpec((None, tq, IQW), lambda i, j: (i, j, OFF_IQ // IQW)),
            pl.BlockSpec((None, s, KV_RANK), lambda i, j: (i, 0, OFF_CKV // KV_RANK)),
            pl.BlockSpec((None, s, LANE), lambda i, j: (i, 0, OFF_IKW // LANE)),
            pl.BlockSpec((None, tq, LANE), lambda i, j: (i, j, OFF_IKW // LANE)),
            pl.BlockSpec((3, ATT_HEADS, tq, tq), lambda i, j: (0, 0, 0, 0)),
            pl.BlockSpec((1, KV_RANK), lambda i, j: (0, 0)),
            pl.BlockSpec((1, KV_RANK), lambda i, j: (0, 0)),
        ],
        out_specs=pl.BlockSpec((None, tq, HR), lambda i, j: (i, j, 0)),
        out_shape=jax.ShapeDtypeStruct((b, s, HR), BF16),
        scratch_shapes=[
            pltpu.VMEM((s, KV_RANK), BF16),
            pltpu.VMEM((s, IDX_DIM), BF16),
            pltpu.VMEM((ATT_HEADS * tq, KV_RANK), BF16),
            pltpu.VMEM((IDX_HEADS * tq, IDX_DIM), BF16),
            pltpu.VMEM((IDX_HEADS, tq, tq), F32),
            pltpu.VMEM((nkb, tq, tq), F32),
            pltpu.VMEM((nkb, ATT_HEADS, tq, tq), F32),
            pltpu.VMEM((ATT_HEADS, tq, tq), F32),
            pltpu.VMEM((ATT_HEADS, tq, tq), F32),
            pltpu.VMEM((ATT_HEADS * tq, KV_RANK), F32),
        ],
        compiler_params=_params(2),
        name="dsa",
    )(p3, p3, p3, p3, p3, bias_tiles, q_gain, kv_gain)


def _mlstm_kernel(mqk_ref, mv_ref, mo_ref, gt_ref, cw_ref, cb_ref, gb_ref, on_ref, o_ref,
                  c_s, n_s, m_s, ext_s):
    lc = M_CHUNK
    dk, dv = M_QK_DIM, M_V_DIM

    @pl.when(pl.program_id(1) == 0)
    def _():
        c_s[...] = jnp.zeros(c_s.shape, F32)
        n_s[...] = jnp.zeros(n_s.shape, F32)
        m_s[...] = jnp.zeros(m_s.shape, F32)
        ext_s[0:8, :] = jnp.zeros((8, MQK), F32)

    x = mqk_ref[...]
    ext_s[8:8 + lc, :] = x
    conv = jnp.broadcast_to(cb_ref[...], (lc, MQK))
    for j in range(CONV_WIDTH):
        lo = 8 - (CONV_WIDTH - 1) + j
        conv = conv + cw_ref[j:j + 1, :] * ext_s[lo:lo + lc, :]
    ext_s[0:8, :] = x[lc - 8:lc, :]
    a = conv * _sigmoid(conv)

    gates = gt_ref[...] + gb_ref[...]
    lsg = _log_sigmoid(gates)
    r_i = lax.broadcasted_iota(jnp.int32, (lc, lc), 0)
    c_i = lax.broadcasted_iota(jnp.int32, (lc, lc), 1)
    tril = r_i >= c_i
    hi = lax.Precision.HIGHEST
    bcum_col = jnp.dot(tril.astype(F32), lsg, precision=hi, preferred_element_type=F32)
    pad = jnp.zeros((LANE - lc, LANE), F32)
    gates_t = jnp.concatenate([gates, pad], axis=0).T[:, :lc]
    lsg_t = jnp.concatenate([lsg, pad], axis=0).T[:, :lc]
    bcum_row = jnp.dot(lsg_t, (r_i <= c_i).astype(F32), precision=hi, preferred_element_type=F32)

    for h in range(M_HEADS):
        bc = bcum_col[:, M_HEADS + h:M_HEADS + h + 1]
        br = bcum_row[M_HEADS + h:M_HEADS + h + 1, :]
        ic = gates[:, h:h + 1]
        ir = gates_t[h:h + 1, :]
        mst = m_s[h, 0:1, 0:1]
        q = a[:, h * dk:(h + 1) * dk]
        k = a[:, MQK // 2 + h * dk:MQK // 2 + (h + 1) * dk] * (dk ** -0.5)
        v = mv_ref[:, h * dv:(h + 1) * dv]
        qb16, kb16, vb16 = q.astype(BF16), k.astype(BF16), v.astype(BF16)

        dmat = jnp.where(tril, bc - br + ir, -jnp.inf)
        inter = bc + mst
        m_row = jnp.maximum(inter, jnp.max(dmat, axis=-1, keepdims=True))
        w_inter = jnp.exp(inter - m_row)
        qk = lax.dot_general(qb16, kb16, (((1,), (1,)), ((), ())), preferred_element_type=F32)
        qk = qk * jnp.exp(dmat - m_row)
        num = (w_inter * jnp.dot(qb16, c_s[h].astype(BF16), preferred_element_type=F32)
               + jnp.dot(qk.astype(BF16), vb16, preferred_element_type=F32))
        den = (w_inter * jnp.sum(q * n_s[h, 0:1, :], axis=-1, keepdims=True)
               + jnp.sum(qk, axis=-1, keepdims=True))
        hh = num / jnp.maximum(jnp.abs(den), jnp.exp(-m_row))

        b_last = bc[lc - 1:lc, :]
        g_col = b_last - bc + ic
        g_row = b_last - br + ir
        m_new = jnp.maximum(b_last + mst, jnp.max(g_row, axis=-1, keepdims=True))
        decay = jnp.exp(b_last + mst - m_new)
        kw = jnp.exp(g_col - m_new) * k
        c_s[h] = decay * c_s[h] + lax.dot_general(kw.astype(BF16), vb16, (((0,), (0,)), ((), ())),
                                                   preferred_element_type=F32)
        n_s[h, 0:1, :] = decay * n_s[h, 0:1, :] + jnp.sum(kw, axis=0, keepdims=True)
        m_s[h] = jnp.broadcast_to(m_new, (8, LANE))

        hn = _rms(hh, on_ref[:, h * dv:(h + 1) * dv])
        o_ref[:, h * dv:(h + 1) * dv] = (hn * _sigmoid(mo_ref[:, h * dv:(h + 1) * dv])).astype(o_ref.dtype)


def _mlstm(p3, conv_w, conv_b, gate_b, out_gain):
    b, s, _ = p3.shape
    lc = M_CHUNK
    return pl.pallas_call(
        _mlstm_kernel,
        grid=(b, s // lc),
        in_specs=[
            pl.BlockSpec((None, lc, MQK), lambda i, j: (i, j, OFF_MQK // MQK)),
            pl.BlockSpec((None, lc, MV), lambda i, j: (i, j, OFF_MV // MV)),
            pl.BlockSpec((None, lc, MV), lambda i, j: (i, j, OFF_MO // MV)),
            pl.BlockSpec((None, lc, LANE), lambda i, j: (i, j, OFF_MG // LANE)),
            pl.BlockSpec((CONV_WIDTH, MQK), lambda i, j: (0, 0)),
            pl.BlockSpec((1, MQK), lambda i, j: (0, 0)),
            pl.BlockSpec((1, LANE), lambda i, j: (0, 0)),
            pl.BlockSpec((1, MV), lambda i, j: (0, 0)),
        ],
        out_specs=pl.BlockSpec((None, lc, MV), lambda i, j: (i, j, 0)),
        out_shape=jax.ShapeDtypeStruct((b, s, MV), BF16),
        scratch_shapes=[
            pltpu.VMEM((M_HEADS, M_QK_DIM, M_V_DIM), F32),
            pltpu.VMEM((M_HEADS, 8, M_QK_DIM), F32),
            pltpu.VMEM((M_HEADS, 8, LANE), F32),
            pltpu.VMEM((8 + lc, MQK), F32),
        ],
        compiler_params=_params(2),
        name="mlstm",
    )(p3, p3, p3, p3, conv_w, conv_b, gate_b, out_gain)


def _merge_kernel(oa_ref, hm_ref, wa_ref, wm_ref, ga_ref, gm_ref, o_ref):
    ya = jnp.dot(oa_ref[...], wa_ref[...], preferred_element_type=F32)
    ym = jnp.dot(hm_ref[...], wm_ref[...], preferred_element_type=F32)
    o_ref[...] = (_sigmoid(ga_ref[...]) * ya + _sigmoid(gm_ref[...]) * ym).astype(o_ref.dtype)


def _merge(oa, hm, wa, wm, p2):
    m = oa.shape[0]
    d = wa.shape[1]
    tm = _tile(m, 1024)
    tn = _tile(d, 512)
    ga0 = OFF_GA // tn
    gm0 = (OFF_GA + d) // tn
    return pl.pallas_call(
        _merge_kernel,
        grid=(m // tm, d // tn),
        in_specs=[
            pl.BlockSpec((tm, HR), lambda i, j: (i, 0)),
            pl.BlockSpec((tm, MV), lambda i, j: (i, 0)),
            pl.BlockSpec((HR, tn), lambda i, j: (0, j)),
            pl.BlockSpec((MV, tn), lambda i, j: (0, j)),
            pl.BlockSpec((tm, tn), lambda i, j: (i, ga0 + j)),
            pl.BlockSpec((tm, tn), lambda i, j: (i, gm0 + j)),
        ],
        out_specs=pl.BlockSpec((tm, tn), lambda i, j: (i, j)),
        out_shape=jax.ShapeDtypeStruct((m, d), BF16),
        compiler_params=_params(2),
        name="merge",
    )(oa, hm, wa, wm, p2, p2)


def _outproj_kernel(x_ref, w_ref, r_ref, o_ref):
    o_ref[...] = r_ref[...] + jnp.dot(x_ref[...], w_ref[...], preferred_element_type=F32)


def _outproj(x, w, res):
    m, k = x.shape
    n = w.shape[1]
    tm = _tile(m, 1024)
    tn = _tile(n, 512)
    return pl.pallas_call(
        _outproj_kernel,
        grid=(m // tm, n // tn),
        in_specs=[
            pl.BlockSpec((tm, k), lambda i, j: (i, 0)),
            pl.BlockSpec((k, tn), lambda i, j: (0, j)),
            pl.BlockSpec((tm, tn), lambda i, j: (i, j)),
        ],
        out_specs=pl.BlockSpec((tm, tn), lambda i, j: (i, j)),
        out_shape=jax.ShapeDtypeStruct((m, n), F32),
        compiler_params=_params(2),
        name="outproj",
    )(x, w, res)


def _bias_tiles(rel_bias):
    tq = Q_BLOCK
    i = jnp.arange(tq)[:, None]
    j = jnp.arange(tq)[None, :]

    def bucket(rel):
        n = jnp.maximum(rel, 0)
        max_exact = REL_BUCKETS // 2
        n_large = jnp.maximum(n, max_exact).astype(F32)
        large = max_exact + (jnp.log(n_large / max_exact) / math.log(REL_MAX_DIST / max_exact)
                             * (REL_BUCKETS - max_exact)).astype(jnp.int32)
        large = jnp.minimum(large, REL_BUCKETS - 1)
        return jnp.where(n < max_exact, n, large)

    tiles = [rel_bias[bucket(i - j + d * tq)] for d in range(3)]
    return jnp.transpose(jnp.stack(tiles), (0, 3, 1, 2)).astype(F32)


def _prep_ffn(w_gu, w_down):
    f = w_down.shape[0]
    fp = -(-f // 512) * 512
    wg = jnp.pad(w_gu[:, :f].astype(BF16), ((0, 0), (0, fp - f)))
    wu = jnp.pad(w_gu[:, f:].astype(BF16), ((0, 0), (0, fp - f)))
    wd = jnp.pad(w_down.astype(BF16), ((0, fp - f), (0, 0)))
    return wg, wu, wd


def _prep_w_in(w_in, d):
    widths = (HR, KV_RANK, IQW, IDX_DIM, IDX_HEADS, MQK, MV, MV, M_HEADS, M_HEADS, d, d)
    segs = []
    off = 0
    for wdt in widths:
        segs.append(w_in[:, off:off + wdt])
        off += wdt
    aq, ckv, iq, ik, iw, mqk, mv, mo, mi, mf, ga, gm = segs
    z = lambda n: jnp.zeros((w_in.shape[0], n), w_in.dtype)
    cat = jnp.concatenate(
        [aq, iq, ckv, ik, iw, z(LANE - IDX_DIM - IDX_HEADS), mi, mf, z(LANE - 2 * M_HEADS), mqk, mv, mo, ga, gm],
        axis=1)
    return cat.astype(BF16)


def kernel(x, rel_bias, ffn1_norm, ffn1_w_gu, ffn1_w_down, mix_norm, w_in, q_norm, kv_norm, conv_w, conv_b,
           igate_b, fgate_b, m_out_norm, w_att_out, w_mem_out, w_out, ffn2_norm, ffn2_w_gu, ffn2_w_down):
    b, s, d = x.shape
    depth = w_in.shape[0]
    topk = min(TOPK_MAX, s // 4)
    m = b * s
    h = x.reshape(m, d)
    bias_tiles = _bias_tiles(rel_bias)
    for l in range(depth):
        h = _ffn(h, ffn1_norm[l][None], *_prep_ffn(ffn1_w_gu[l], ffn1_w_down[l]))

        p2 = _inproj(h, mix_norm[l][None], _prep_w_in(w_in[l], d))
        p3 = p2.reshape(b, s, p2.shape[1])
        o_att = _dsa(p3, bias_tiles, q_norm[l][None], kv_norm[l][None], topk)
        gate_b = jnp.concatenate([igate_b[l], fgate_b[l], jnp.zeros((LANE - 2 * M_HEADS,), F32)])[None]
        hm = _mlstm(p3, conv_w[l], conv_b[l][None], gate_b, m_out_norm[l].reshape(1, MV))
        merged = _merge(o_att.reshape(m, HR), hm.reshape(m, MV),
                        w_att_out[l].reshape(HR, d).astype(BF16), w_mem_out[l].astype(BF16), p2)
        h = _outproj(merged, w_out[l].astype(BF16), h)

        h = _ffn(h, ffn2_norm[l][None], *_prep_ffn(ffn2_w_gu[l], ffn2_w_down[l]))
    return h.reshape(b, s, d)
```

```python
import functools
import math

import jax
import jax.numpy as jnp
from jax import lax
from jax.experimental import pallas as pl
from jax.experimental.pallas import tpu as pltpu

F32 = jnp.float32
BF16 = jnp.bfloat16

ATT_HEADS = 8
KV_RANK = 256
IDX_HEADS = 8
IDX_DIM = 64
TOPK_MAX = 256
Q_BLOCK = 128
REL_BUCKETS = 32
REL_MAX_DIST = 128
M_HEADS = 4
M_QK_DIM = 128
M_V_DIM = 256
M_CHUNK = 64
CONV_WIDTH = 4
EPS = 1e-6

HR = ATT_HEADS * KV_RANK
IQW = IDX_HEADS * IDX_DIM
MQK = 2 * M_HEADS * M_QK_DIM
MV = M_HEADS * M_V_DIM
LANE = 128
VMEM_LIMIT = 56 * 1024 * 1024

OFF_AQ = 0
OFF_IQ = OFF_AQ + HR
OFF_CKV = OFF_IQ + IQW
OFF_IKW = OFF_CKV + KV_RANK
OFF_MG = OFF_IKW + LANE
OFF_MQK = OFF_MG + LANE
OFF_MV = OFF_MQK + MQK
OFF_MO = OFF_MV + MV
OFF_GA = OFF_MO + MV

KEY_NEG_INF = -2139095041


def _params(n_grid):
    return pltpu.CompilerParams(dimension_semantics=("arbitrary",) * n_grid,
                                vmem_limit_bytes=VMEM_LIMIT)


def _tile(n, pref):
    if n <= pref:
        return n
    t = (pref // LANE) * LANE
    while n % t:
        t -= LANE
    return t


def _rms(x, g):
    return x * lax.rsqrt(jnp.mean(x * x, axis=-1, keepdims=True) + EPS) * g


def _sigmoid(x):
    return 1.0 / (1.0 + jnp.exp(-x))


def _log_sigmoid(x):
    return jnp.minimum(x, 0.0) - jnp.log(1.0 + jnp.exp(-jnp.abs(x)))


def _ffn_kernel(h_ref, g_ref, wg_ref, wu_ref, wd_ref, o_ref, xn_ref):
    j = pl.program_id(1)

    @pl.when(j == 0)
    def _():
        x = h_ref[...]
        xn_ref[...] = _rms(x, g_ref[...]).astype(BF16)
        o_ref[...] = x

    xn = xn_ref[...]
    g = jnp.dot(xn, wg_ref[...], preferred_element_type=F32)
    u = jnp.dot(xn, wu_ref[...], preferred_element_type=F32)
    a = (g * _sigmoid(g) * u * 0.5).astype(BF16)
    o_ref[...] += jnp.dot(a, wd_ref[...], preferred_element_type=F32)


def _ffn(h, gain, wg, wu, wd):
    m, d = h.shape
    fp = wg.shape[1]
    tm = _tile(m, 512)
    tf = _tile(fp, 512)
    return pl.pallas_call(
        _ffn_kernel,
        grid=(m // tm, fp // tf),
        in_specs=[
            pl.BlockSpec((tm, d), lambda i, j: (i, 0)),
            pl.BlockSpec((1, d), lambda i, j: (0, 0)),
            pl.BlockSpec((d, tf), lambda i, j: (0, j)),
            pl.BlockSpec((d, tf), lambda i, j: (0, j)),
            pl.BlockSpec((tf, d), lambda i, j: (j, 0)),
        ],
        out_specs=pl.BlockSpec((tm, d), lambda i, j: (i, 0)),
        out_shape=jax.ShapeDtypeStruct((m, d), F32),
        scratch_shapes=[pltpu.VMEM((tm, d), BF16)],
        compiler_params=_params(2),
        name="ffn",
    )(h, gain, wg, wu, wd)


def _inproj_kernel(h_ref, g_ref, w_ref, o_ref, xn_ref):
    @pl.when(pl.program_id(1) == 0)
    def _():
        xn_ref[...] = _rms(h_ref[...], g_ref[...]).astype(BF16)

    o_ref[...] = jnp.dot(xn_ref[...], w_ref[...], preferred_element_type=F32)


def _inproj(h, gain, w):
    m, d = h.shape
    n = w.shape[1]
    tm = _tile(m, 1024)
    tn = _tile(n, 512)
    return pl.pallas_call(
        _inproj_kernel,
        grid=(m // tm, n // tn),
        in_specs=[
            pl.BlockSpec((tm, d), lambda i, j: (i, 0)),
            pl.BlockSpec((1, d), lambda i, j: (0, 0)),
            pl.BlockSpec((d, tn), lambda i, j: (0, j)),
        ],
        out_specs=pl.BlockSpec((tm, tn), lambda i, j: (i, j)),
        out_shape=jax.ShapeDtypeStruct((m, n), F32),
        scratch_shapes=[pltpu.VMEM((tm, d), BF16)],
        compiler_params=_params(2),
        name="inproj",
    )(h, gain, w)


DSA_TQ = 256
DSA_KC = 256


def _dsa_kernel(aq_ref, iq_ref, ckv_ref, ik_ref, iw_ref, bias_ref, qn_ref, kn_ref, o_ref,
                ckv_s, ik_s, q_s, iq_s, wb_s, sc_s, sct_s, lg_s, m_s, l_s, acc_s, *, topk):
    qi = pl.program_id(1)
    tq, kc, nh, sub = DSA_TQ, DSA_KC, ATT_HEADS, Q_BLOCK
    nchunk = qi + 1

    @pl.when(qi == 0)
    def _():
        ckv_s[...] = _rms(ckv_ref[...], kn_ref[...]).astype(BF16)
        ik_s[...] = ik_ref[:, :IDX_DIM].astype(BF16)

    scale = KV_RANK ** -0.5
    for h in range(nh):
        qh = _rms(aq_ref[:, h * KV_RANK:(h + 1) * KV_RANK], qn_ref[...]) * scale
        q_s[h * tq:(h + 1) * tq, :] = qh.astype(BF16)
    for h in range(IDX_HEADS):
        iq_s[h * tq:(h + 1) * tq, :] = iq_ref[:, h * IDX_DIM:(h + 1) * IDX_DIM].astype(BF16)
    w = iw_ref[:, IDX_DIM:IDX_DIM + IDX_HEADS] * (IQW ** -0.5)
    for h in range(IDX_HEADS):
        wb_s[h] = jnp.broadcast_to(w[:, h:h + 1], (tq, LANE))

    row = lax.broadcasted_iota(jnp.int32, (tq, kc), 0)
    col = lax.broadcasted_iota(jnp.int32, (tq, kc), 1)
    nt_dims = (((1,), (1,)), ((), ()))

    def causal_mask(c):
        return (c * kc + col) <= (qi * tq + row)

    def score_body(c, carry):
        off = pl.multiple_of(c * kc, kc)
        d = lax.dot_general(iq_s[...], ik_s[pl.ds(off, kc), :], nt_dims, preferred_element_type=F32)
        d = jnp.maximum(d, 0.0).reshape(IDX_HEADS, tq, kc)
        wb = wb_s[...]
        s = jnp.sum(d * jnp.concatenate([wb, wb], axis=-1), axis=0)
        s = jnp.where(causal_mask(c), s, -jnp.inf)
        sc_s[c] = s
        sct_s[c] = s.T
        return carry

    lax.fori_loop(0, nchunk, score_body, 0)

    kf = float(topk)

    def bit_body(it, t_key):
        cand = t_key + lax.shift_left(jnp.int32(1), 31 - it)
        cbits = jnp.where(cand >= 0, cand, cand ^ jnp.int32(0x7FFFFFFF))
        cf = lax.bitcast_convert_type(cbits, F32)

        def chunk_body(c, acc):
            st = sct_s[c].reshape(kc // 32, 4, 8, tq)
            hit = jnp.where(st >= cf[None, None], 1.0, 0.0)
            return acc + jnp.sum(jnp.sum(hit, axis=1), axis=0)

        acc = lax.fori_loop(0, nchunk, chunk_body, jnp.zeros((8, tq), F32))
        cnt = jnp.sum(acc, axis=0, keepdims=True)
        ok = (cnt >= kf) | (cand < KEY_NEG_INF)
        return jnp.where(ok, cand, t_key)

    t_key = lax.fori_loop(0, 32, bit_body, jnp.full((8, tq), -2 ** 31, jnp.int32))
    t_bits = jnp.where(t_key >= 0, t_key, t_key ^ jnp.int32(0x7FFFFFFF))
    thr_row = lax.bitcast_convert_type(t_bits, F32)[0:1, :]
    thr = jnp.broadcast_to(thr_row, (kc, tq)).T

    m_s[...] = jnp.full(m_s.shape, -jnp.inf, F32)

    def bias_tile(c, a, b):
        return bias_ref[jnp.clip(2 * qi + a - 2 * c - b, 0, 2)]

    def logit_body(c, carry):
        off = pl.multiple_of(c * kc, kc)
        lg = lax.dot_general(q_s[...], ckv_s[pl.ds(off, kc), :], nt_dims, preferred_element_type=F32)
        bias = jnp.concatenate(
            [jnp.concatenate([bias_tile(c, a, 0), bias_tile(c, a, 1)], axis=-1) for a in range(tq // sub)],
            axis=-2)
        keep = (sc_s[c] >= thr) & causal_mask(c)
        lg = jnp.where(keep[None], lg.reshape(nh, tq, kc) + bias, -jnp.inf)
        lg_s[c] = lg
        m_s[...] = jnp.maximum(m_s[...], jnp.maximum(lg[..., :LANE], lg[..., LANE:]))
        return carry

    lax.fori_loop(0, nchunk, logit_body, 0)

    m = jnp.max(m_s[...], axis=-1, keepdims=True)
    m_s[...] = jnp.broadcast_to(m, m_s.shape)
    l_s[...] = jnp.zeros(l_s.shape, F32)
    acc_s[...] = jnp.zeros(acc_s.shape, F32)

    def pv_body(c, carry):
        off = pl.multiple_of(c * kc, kc)
        lg = lg_s[c]
        mb = m_s[...]
        p0 = jnp.exp(lg[..., :LANE] - mb)
        p1 = jnp.exp(lg[..., LANE:] - mb)
        l_s[...] += p0 + p1
        p = jnp.concatenate([p0, p1], axis=-1).reshape(nh * tq, kc).astype(BF16)
        acc_s[...] += jnp.dot(p, ckv_s[pl.ds(off, kc), :], preferred_element_type=F32)
        return carry

    lax.fori_loop(0, nchunk, pv_body, 0)

    lsum = jnp.sum(l_s[...], axis=-1, keepdims=True)
    for h in range(nh):
        o_ref[:, h * KV_RANK:(h + 1) * KV_RANK] = (acc_s[h * tq:(h + 1) * tq, :] / lsum[h]).astype(o_ref.dtype)


def _dsa(p3, bias_tiles, q_gain, kv_gain, topk):
    b, s, _ = p3.shape
    tq, kc, sub = DSA_TQ, DSA_KC, Q_BLOCK
    assert tq == kc and s % tq == 0
    nkc = s // kc
    kern = functools.partial(_dsa_kernel, topk=topk)
    return pl.pallas_call(
        kern,
        grid=(b, s // tq),
        in_specs=[
            pl.BlockSpec((None, tq, HR), lambda i, j: (i, j, OFF_AQ // HR)),
            pl.BlockSpec((None, tq, IQW), lambda i, j: (i, j, OFF_IQ // IQW)),
            pl.BlockSpec((None, s, KV_RANK), lambda i, j: (i, 0, OFF_CKV // KV_RANK)),
            pl.BlockSpec((None, s, LANE), lambda i, j: (i, 0, OFF_IKW // LANE)),
            pl.BlockSpec((None, tq, LANE), lambda i, j: (i, j, OFF_IKW // LANE)),
            pl.BlockSpec((3, ATT_HEADS, sub, sub), lambda i, j: (0, 0, 0, 0)),
            pl.BlockSpec((1, KV_RANK), lambda i, j: (0, 0)),
            pl.BlockSpec((1, KV_RANK), lambda i, j: (0, 0)),
        ],
        out_specs=pl.BlockSpec((None, tq, HR), lambda i, j: (i, j, 0)),
        out_shape=jax.ShapeDtypeStruct((b, s, HR), BF16),
        scratch_shapes=[
            pltpu.VMEM((s, KV_RANK), BF16),
            pltpu.VMEM((s, IDX_DIM), BF16),
            pltpu.VMEM((ATT_HEADS * tq, KV_RANK), BF16),
            pltpu.VMEM((IDX_HEADS * tq, IDX_DIM), BF16),
            pltpu.VMEM((IDX_HEADS, tq, LANE), F32),
            pltpu.VMEM((nkc, tq, kc), F32),
            pltpu.VMEM((nkc, kc, tq), F32),
            pltpu.VMEM((nkc, ATT_HEADS, tq, kc), F32),
            pltpu.VMEM((ATT_HEADS, tq, LANE), F32),
            pltpu.VMEM((ATT_HEADS, tq, LANE), F32),
            pltpu.VMEM((ATT_HEADS * tq, KV_RANK), F32),
        ],
        compiler_params=_params(2),
        name="dsa",
    )(p3, p3, p3, p3, p3, bias_tiles, q_gain, kv_gain)


def _mlstm_kernel(mqk_ref, mv_ref, mo_ref, gt_ref, cw_ref, cb_ref, gb_ref, on_ref, o_ref,
                  c_s, n_s, m_s, ext_s):
    lc = M_CHUNK
    dk, dv = M_QK_DIM, M_V_DIM

    @pl.when(pl.program_id(1) == 0)
    def _():
        c_s[...] = jnp.zeros(c_s.shape, F32)
        n_s[...] = jnp.zeros(n_s.shape, F32)
        m_s[...] = jnp.zeros(m_s.shape, F32)
        ext_s[0:8, :] = jnp.zeros((8, MQK), F32)

    x = mqk_ref[...]
    ext_s[8:8 + lc, :] = x
    conv = jnp.broadcast_to(cb_ref[...], (lc, MQK))
    for j in range(CONV_WIDTH):
        lo = 8 - (CONV_WIDTH - 1) + j
        conv = conv + cw_ref[j:j + 1, :] * ext_s[lo:lo + lc, :]
    ext_s[0:8, :] = x[lc - 8:lc, :]
    a = conv * _sigmoid(conv)

    gates = gt_ref[...] + gb_ref[...]
    lsg = _log_sigmoid(gates)
    r_i = lax.broadcasted_iota(jnp.int32, (lc, lc), 0)
    c_i = lax.broadcasted_iota(jnp.int32, (lc, lc), 1)
    tril = r_i >= c_i
    hi = lax.Precision.HIGHEST
    bcum_col = jnp.dot(tril.astype(F32), lsg, precision=hi, preferred_element_type=F32)
    pad = jnp.zeros((LANE - lc, LANE), F32)
    gates_t = jnp.concatenate([gates, pad], axis=0).T[:, :lc]
    lsg_t = jnp.concatenate([lsg, pad], axis=0).T[:, :lc]
    bcum_row = jnp.dot(lsg_t, (r_i <= c_i).astype(F32), precision=hi, preferred_element_type=F32)

    for h in range(M_HEADS):
        bc = bcum_col[:, M_HEADS + h:M_HEADS + h + 1]
        br = bcum_row[M_HEADS + h:M_HEADS + h + 1, :]
        ic = gates[:, h:h + 1]
        ir = gates_t[h:h + 1, :]
        mst = m_s[h, 0:1, 0:1]
        q = a[:, h * dk:(h + 1) * dk]
        k = a[:, MQK // 2 + h * dk:MQK // 2 + (h + 1) * dk] * (dk ** -0.5)
        v = mv_ref[:, h * dv:(h + 1) * dv]
        qb16, kb16, vb16 = q.astype(BF16), k.astype(BF16), v.astype(BF16)

        dmat = jnp.where(tril, bc - br + ir, -jnp.inf)
        inter = bc + mst
        m_row = jnp.maximum(inter, jnp.max(dmat, axis=-1, keepdims=True))
        w_inter = jnp.exp(inter - m_row)
        qk = lax.dot_general(qb16, kb16, (((1,), (1,)), ((), ())), preferred_element_type=F32)
        qk = qk * jnp.exp(dmat - m_row)
        num = (w_inter * jnp.dot(qb16, c_s[h].astype(BF16), preferred_element_type=F32)
               + jnp.dot(qk.astype(BF16), vb16, preferred_element_type=F32))
        den = (w_inter * jnp.sum(q * n_s[h, 0:1, :], axis=-1, keepdims=True)
               + jnp.sum(qk, axis=-1, keepdims=True))
        hh = num / jnp.maximum(jnp.abs(den), jnp.exp(-m_row))

        b_last = bc[lc - 1:lc, :]
        g_col = b_last - bc + ic
        g_row = b_last - br + ir
        m_new = jnp.maximum(b_last + mst, jnp.max(g_row, axis=-1, keepdims=True))
        decay = jnp.exp(b_last + mst - m_new)
        kw = jnp.exp(g_col - m_new) * k
        c_s[h] = decay * c_s[h] + lax.dot_general(kw.astype(BF16), vb16, (((0,), (0,)), ((), ())),
                                                   preferred_element_type=F32)
        n_s[h, 0:1, :] = decay * n_s[h, 0:1, :] + jnp.sum(kw, axis=0, keepdims=True)
        m_s[h] = jnp.broadcast_to(m_new, (8, LANE))

        hn = _rms(hh, on_ref[:, h * dv:(h + 1) * dv])
        o_ref[:, h * dv:(h + 1) * dv] = (hn * _sigmoid(mo_ref[:, h * dv:(h + 1) * dv])).astype(o_ref.dtype)


def _mlstm(p3, conv_w, conv_b, gate_b, out_gain):
    b, s, _ = p3.shape
    lc = M_CHUNK
    return pl.pallas_call(
        _mlstm_kernel,
        grid=(b, s // lc),
        in_specs=[
            pl.BlockSpec((None, lc, MQK), lambda i, j: (i, j, OFF_MQK // MQK)),
            pl.BlockSpec((None, lc, MV), lambda i, j: (i, j, OFF_MV // MV)),
            pl.BlockSpec((None, lc, MV), lambda i, j: (i, j, OFF_MO // MV)),
            pl.BlockSpec((None, lc, LANE), lambda i, j: (i, j, OFF_MG // LANE)),
            pl.BlockSpec((CONV_WIDTH, MQK), lambda i, j: (0, 0)),
            pl.BlockSpec((1, MQK), lambda i, j: (0, 0)),
            pl.BlockSpec((1, LANE), lambda i, j: (0, 0)),
            pl.BlockSpec((1, MV), lambda i, j: (0, 0)),
        ],
        out_specs=pl.BlockSpec((None, lc, MV), lambda i, j: (i, j, 0)),
        out_shape=jax.ShapeDtypeStruct((b, s, MV), BF16),
        scratch_shapes=[
            pltpu.VMEM((M_HEADS, M_QK_DIM, M_V_DIM), F32),
            pltpu.VMEM((M_HEADS, 8, M_QK_DIM), F32),
            pltpu.VMEM((M_HEADS, 8, LANE), F32),
            pltpu.VMEM((8 + lc, MQK), F32),
        ],
        compiler_params=_params(2),
        name="mlstm",
    )(p3, p3, p3, p3, conv_w, conv_b, gate_b, out_gain)


def _merge_kernel(oa_ref, hm_ref, wa_ref, wm_ref, ga_ref, gm_ref, o_ref):
    ya = jnp.dot(oa_ref[...], wa_ref[...], preferred_element_type=F32)
    ym = jnp.dot(hm_ref[...], wm_ref[...], preferred_element_type=F32)
    o_ref[...] = (_sigmoid(ga_ref[...]) * ya + _sigmoid(gm_ref[...]) * ym).astype(o_ref.dtype)


def _merge(oa, hm, wa, wm, p2):
    m = oa.shape[0]
    d = wa.shape[1]
    tm = _tile(m, 1024)
    tn = _tile(d, 512)
    ga0 = OFF_GA // tn
    gm0 = (OFF_GA + d) // tn
    return pl.pallas_call(
        _merge_kernel,
        grid=(m // tm, d // tn),
        in_specs=[
            pl.BlockSpec((tm, HR), lambda i, j: (i, 0)),
            pl.BlockSpec((tm, MV), lambda i, j: (i, 0)),
            pl.BlockSpec((HR, tn), lambda i, j: (0, j)),
            pl.BlockSpec((MV, tn), lambda i, j: (0, j)),
            pl.BlockSpec((tm, tn), lambda i, j: (i, ga0 + j)),
            pl.BlockSpec((tm, tn), lambda i, j: (i, gm0 + j)),
        ],
        out_specs=pl.BlockSpec((tm, tn), lambda i, j: (i, j)),
        out_shape=jax.ShapeDtypeStruct((m, d), BF16),
        compiler_params=_params(2),
        name="merge",
    )(oa, hm, wa, wm, p2, p2)


def _outproj_kernel(x_ref, w_ref, r_ref, o_ref):
    o_ref[...] = r_ref[...] + jnp.dot(x_ref[...], w_ref[...], preferred_element_type=F32)


def _outproj(x, w, res):
    m, k = x.shape
    n = w.shape[1]
    tm = _tile(m, 1024)
    tn = _tile(n, 512)
    return pl.pallas_call(
        _outproj_kernel,
        grid=(m // tm, n // tn),
        in_specs=[
            pl.BlockSpec((tm, k), lambda i, j: (i, 0)),
            pl.BlockSpec((k, tn), lambda i, j: (0, j)),
            pl.BlockSpec((tm, tn), lambda i, j: (i, j)),
        ],
        out_specs=pl.BlockSpec((tm, tn), lambda i, j: (i, j)),
        out_shape=jax.ShapeDtypeStruct((m, n), F32),
        compiler_params=_params(2),
        name="outproj",
    )(x, w, res)


def _bias_tiles(rel_bias):
    tq = Q_BLOCK
    i = jnp.arange(tq)[:, None]
    j = jnp.arange(tq)[None, :]

    def bucket(rel):
        n = jnp.maximum(rel, 0)
        max_exact = REL_BUCKETS // 2
        n_large = jnp.maximum(n, max_exact).astype(F32)
        large = max_exact + (jnp.log(n_large / max_exact) / math.log(REL_MAX_DIST / max_exact)
                             * (REL_BUCKETS - max_exact)).astype(jnp.int32)
        large = jnp.minimum(large, REL_BUCKETS - 1)
        return jnp.where(n < max_exact, n, large)

    onehot = jax.nn.one_hot(jnp.stack([bucket(i - j + d * tq) for d in range(3)]), REL_BUCKETS, dtype=F32)
    return jnp.einsum("dijk,kh->dhij", onehot, rel_bias.astype(F32), precision=lax.Precision.HIGHEST)


def _prep_ffn(w_gu, w_down):
    f = w_down.shape[0]
    fp = -(-f // 512) * 512
    wg = jnp.pad(w_gu[:, :f].astype(BF16), ((0, 0), (0, fp - f)))
    wu = jnp.pad(w_gu[:, f:].astype(BF16), ((0, 0), (0, fp - f)))
    wd = jnp.pad(w_down.astype(BF16), ((0, fp - f), (0, 0)))
    return wg, wu, wd


def _prep_w_in(w_in, d):
    widths = (HR, KV_RANK, IQW, IDX_DIM, IDX_HEADS, MQK, MV, MV, M_HEADS, M_HEADS, d, d)
    segs = []
    off = 0
    for wdt in widths:
        segs.append(w_in[:, off:off + wdt])
        off += wdt
    aq, ckv, iq, ik, iw, mqk, mv, mo, mi, mf, ga, gm = segs
    z = lambda n: jnp.zeros((w_in.shape[0], n), w_in.dtype)
    cat = jnp.concatenate(
        [aq, iq, ckv, ik, iw, z(LANE - IDX_DIM - IDX_HEADS), mi, mf, z(LANE - 2 * M_HEADS), mqk, mv, mo, ga, gm],
        axis=1)
    return cat.astype(BF16)


def kernel(x, rel_bias, ffn1_norm, ffn1_w_gu, ffn1_w_down, mix_norm, w_in, q_norm, kv_norm, conv_w, conv_b,
           igate_b, fgate_b, m_out_norm, w_att_out, w_mem_out, w_out, ffn2_norm, ffn2_w_gu, ffn2_w_down):
    b, s, d = x.shape
    depth = w_in.shape[0]
    topk = min(TOPK_MAX, s // 4)
    m = b * s
    h = x.reshape(m, d)
    bias_tiles = _bias_tiles(rel_bias)
    for l in range(depth):
        h = _ffn(h, ffn1_norm[l][None], *_prep_ffn(ffn1_w_gu[l], ffn1_w_down[l]))

        p2 = _inproj(h, mix_norm[l][None], _prep_w_in(w_in[l], d))
        p3 = p2.reshape(b, s, p2.shape[1])
        o_att = _dsa(p3, bias_tiles, q_norm[l][None], kv_norm[l][None], topk)
        gate_b = jnp.concatenate([igate_b[l], fgate_b[l], jnp.zeros((LANE - 2 * M_HEADS,), F32)])[None]
        hm = _mlstm(p3, conv_w[l], conv_b[l][None], gate_b, m_out_norm[l].reshape(1, MV))
        merged = _merge(o_att.reshape(m, HR), hm.reshape(m, MV),
                        w_att_out[l].reshape(HR, d).astype(BF16), w_mem_out[l].astype(BF16), p2)
        h = _outproj(merged, w_out[l].astype(BF16), h)

        h = _ffn(h, ffn2_norm[l][None], *_prep_ffn(ffn2_w_gu[l], ffn2_w_down[l]))
    return h.reshape(b, s, d)
```

```python
import functools
import math

import jax
import jax.numpy as jnp
from jax import lax
from jax.experimental import pallas as pl
from jax.experimental.pallas import tpu as pltpu

F32 = jnp.float32
BF16 = jnp.bfloat16

ATT_HEADS = 8
KV_RANK = 256
IDX_HEADS = 8
IDX_DIM = 64
TOPK_MAX = 256
Q_BLOCK = 128
REL_BUCKETS = 32
REL_MAX_DIST = 128
M_HEADS = 4
M_QK_DIM = 128
M_V_DIM = 256
CONV_WIDTH = 4
EPS = 1e-6

HR = ATT_HEADS * KV_RANK
IQW = IDX_HEADS * IDX_DIM
MQK = 2 * M_HEADS * M_QK_DIM
MV = M_HEADS * M_V_DIM
LANE = 128
SUBLANE = 8
VMEM_LIMIT = 56 * 1024 * 1024

OFF_IQ = 0
OFF_CKV = OFF_IQ + IQW
OFF_IKW = OFF_CKV + KV_RANK
OFF_MG = OFF_IKW + LANE
OFF_MQK = OFF_MG + LANE
OFF_MV = OFF_MQK + MQK
OFF_MO = OFF_MV + MV
OFF_GA = OFF_MO + MV

KEY_NEG_INF = -2139095041
LOG2E = math.log2(math.e)
Q_SCALE = KV_RANK ** -0.5 * LOG2E
F32_LOWEST = float(jnp.finfo(jnp.float32).min)


def _params(n_grid):
    return pltpu.CompilerParams(dimension_semantics=("arbitrary",) * n_grid,
                                vmem_limit_bytes=VMEM_LIMIT)


def _tile(n, pref):
    if n <= pref:
        return n
    t = (pref // LANE) * LANE
    while n % t:
        t -= LANE
    return t


def _rms(x, g):
    return x * lax.rsqrt(jnp.mean(x * x, axis=-1, keepdims=True) + EPS) * g


def _sigmoid(x):
    return 1.0 / (1.0 + jnp.exp(-x))


def _log_sigmoid(x):
    return jnp.minimum(x, 0.0) - jnp.log(1.0 + jnp.exp(-jnp.abs(x)))


def _ffn_kernel(h_ref, g_ref, wg_ref, wu_ref, wd_ref, o_ref, xn_ref, *, overlap):
    j = pl.program_id(1)

    @pl.when(j == 0)
    def _():
        x = h_ref[...]
        xn_ref[...] = _rms(x, g_ref[...]).astype(BF16)
        o_ref[...] = x

    xn = xn_ref[...]
    g = jnp.dot(xn, wg_ref[0], preferred_element_type=F32)
    u = jnp.dot(xn, wu_ref[0], preferred_element_type=F32)
    a = g * _sigmoid(g) * u * 0.5
    if overlap:
        col = lax.broadcasted_iota(jnp.int32, a.shape, 1)
        a = jnp.where((j == pl.num_programs(1) - 1) & (col < overlap), 0.0, a)
    o_ref[...] += jnp.dot(a.astype(BF16), wd_ref[0], preferred_element_type=F32)


def _ffn(h, gain, w_gu, w_down, l):
    m, d = h.shape
    f = w_down.shape[1]
    tm = _tile(m, 1024)
    tf = min(f, 512)
    nj = -(-f // tf)
    assert f % LANE == 0 and tf % LANE == 0

    def off(j):
        return pl.multiple_of(jnp.minimum(j * tf, f - tf), LANE)

    return pl.pallas_call(
        functools.partial(_ffn_kernel, overlap=nj * tf - f),
        grid=(m // tm, nj),
        in_specs=[
            pl.BlockSpec((tm, d), lambda i, j: (i, 0)),
            pl.BlockSpec((1, d), lambda i, j: (0, 0)),
            pl.BlockSpec((pl.Element(1), pl.Element(d), pl.Element(tf)), lambda i, j: (l, 0, off(j))),
            pl.BlockSpec((pl.Element(1), pl.Element(d), pl.Element(tf)),
                         lambda i, j: (l, 0, pl.multiple_of(f + off(j), LANE))),
            pl.BlockSpec((pl.Element(1), pl.Element(tf), pl.Element(d)), lambda i, j: (l, off(j), 0)),
        ],
        out_specs=pl.BlockSpec((tm, d), lambda i, j: (i, 0)),
        out_shape=jax.ShapeDtypeStruct((m, d), F32),
        scratch_shapes=[pltpu.VMEM((tm, d), BF16)],
        compiler_params=_params(2),
        name="ffn",
    )(h, gain, w_gu, w_gu, w_down)


def _inproj_kernel(h_ref, g_ref, w_ref, qn_ref, q_ref, p_ref, xn_ref, *, nq):
    j = pl.program_id(1)

    @pl.when(j == 0)
    def _():
        xn_ref[...] = _rms(h_ref[...], g_ref[...]).astype(BF16)

    @pl.when(j < nq)
    def _():
        acc = jnp.dot(xn_ref[...], w_ref[...], preferred_element_type=F32)
        for k in range(acc.shape[1] // KV_RANK):
            sl = slice(k * KV_RANK, (k + 1) * KV_RANK)
            q_ref[:, sl] = (_rms(acc[:, sl], qn_ref[...]) * Q_SCALE).astype(q_ref.dtype)

    @pl.when(j >= nq)
    def _():
        p_ref[...] = jnp.dot(xn_ref[...], w_ref[...], preferred_element_type=F32)


def _inproj(h, gain, w, q_gain, l):
    m, d = h.shape
    n = w.shape[2]
    tm = _tile(m, 1024)
    tn = 512
    assert HR % tn == 0 and n % tn == 0 and tn % KV_RANK == 0
    nq = HR // tn
    return pl.pallas_call(
        functools.partial(_inproj_kernel, nq=nq),
        grid=(m // tm, n // tn),
        in_specs=[
            pl.BlockSpec((tm, d), lambda i, j: (i, 0)),
            pl.BlockSpec((1, d), lambda i, j: (0, 0)),
            pl.BlockSpec((None, d, tn), lambda i, j: (l, 0, j)),
            pl.BlockSpec((1, KV_RANK), lambda i, j: (0, 0)),
        ],
        out_specs=[
            pl.BlockSpec((tm, tn), lambda i, j: (i, jnp.minimum(j, nq - 1))),
            pl.BlockSpec((tm, tn), lambda i, j: (i, jnp.maximum(j - nq, 0))),
        ],
        out_shape=[jax.ShapeDtypeStruct((m, HR), BF16), jax.ShapeDtypeStruct((m, n - HR), F32)],
        scratch_shapes=[pltpu.VMEM((tm, d), BF16)],
        compiler_params=_params(2),
        name="inproj",
    )(h, gain, w, q_gain)


DSA_TQ = 256
DSA_KC = 256
DSA_RB = 64


def _dsa_kernel(q_ref, iq_ref, ckv_ref, ik_ref, iw_ref, bias_ref, kn_ref, o_ref,
                ckv_s, ik_s, q_s, iq_s, wb_s, sc_s, sct_s, lg_s, m_s, l_s, acc_s, *, topk):
    qi = pl.program_id(1)
    tq, kc, nh, sub, rb = DSA_TQ, DSA_KC, ATT_HEADS, Q_BLOCK, DSA_RB
    nrb = tq // rb
    nchunk = qi + 1

    @pl.when(qi == 0)
    def _():
        ckv_s[...] = _rms(ckv_ref[...], kn_ref[...]).astype(BF16)
        ik_s[...] = ik_ref[:, :IDX_DIM].astype(BF16)

    for h in range(nh):
        q_s[h * tq:(h + 1) * tq, :] = q_ref[:, h * KV_RANK:(h + 1) * KV_RANK]
    w = iw_ref[:, IDX_DIM:IDX_DIM + IDX_HEADS] * (IQW ** -0.5)
    for r in range(nrb):
        for h in range(IDX_HEADS):
            iq_s[r, h * rb:(h + 1) * rb, :] = iq_ref[r * rb:(r + 1) * rb, h * IDX_DIM:(h + 1) * IDX_DIM].astype(BF16)
            wb_s[r, h] = jnp.broadcast_to(w[r * rb:(r + 1) * rb, h:h + 1], (rb, LANE))

    nt_dims = (((1,), (1,)), ((), ()))
    row_rb = lax.broadcasted_iota(jnp.int32, (rb, kc), 0)
    col_rb = lax.broadcasted_iota(jnp.int32, (rb, kc), 1)

    def score_body(c, carry):
        off = pl.multiple_of(c * kc, kc)
        ikc = ik_s[pl.ds(off, kc), :]
        for r in range(nrb):
            d = lax.dot_general(iq_s[r], ikc, nt_dims, preferred_element_type=F32)
            d = jnp.maximum(d, 0.0).reshape(IDX_HEADS, rb, kc)
            wb = wb_s[r]
            s = jnp.sum(d * jnp.concatenate([wb, wb], axis=-1), axis=0)
            causal = (c * kc + col_rb) <= (qi * tq + r * rb + row_rb)
            sc_s[c, r * rb:(r + 1) * rb, :] = jnp.where(causal, s, -jnp.inf)
        sct_s[c] = sc_s[c].T
        return carry

    lax.fori_loop(0, nchunk, score_body, 0)

    kf = float(topk)

    def count(pred):
        def chunk_body(c, acc):
            hit = jnp.where(pred(sct_s[c].reshape(kc // 32, 4, SUBLANE, tq), c), 1.0, 0.0)
            return acc + jnp.sum(jnp.sum(hit, axis=1), axis=0)

        acc = lax.fori_loop(0, nchunk, chunk_body, jnp.zeros((SUBLANE, tq), F32))
        return jnp.sum(acc, axis=0, keepdims=True)

    def key_to_float(key):
        return lax.bitcast_convert_type(jnp.where(key >= 0, key, key ^ jnp.int32(0x7FFFFFFF)), F32)

    def bit_body(it, t_key):
        cand = t_key + lax.shift_left(jnp.int32(1), 31 - it)
        cf = key_to_float(cand)
        cnt = count(lambda st, c: st >= cf[None, None])
        ok = (cnt >= kf) | (cand < KEY_NEG_INF)
        return jnp.where(ok, cand, t_key)

    t_key = lax.fori_loop(0, 32, bit_body, jnp.full((SUBLANE, tq), -2 ** 31, jnp.int32))
    thr_row = jnp.maximum(key_to_float(t_key), F32_LOWEST)

    def to_rows(v):
        return jnp.broadcast_to(v[0:1, :], (kc, tq)).T

    n_gt = count(lambda st, c: st > thr_row[None, None])
    n_ge = count(lambda st, c: st >= thr_row[None, None])
    need = kf - n_gt

    @pl.when(jnp.max(n_ge - kf) > 0.0)
    def _():
        kidx_t = lax.broadcasted_iota(jnp.int32, (kc // 32, 4, SUBLANE, tq), 0) * 32 \
            + lax.broadcasted_iota(jnp.int32, (kc // 32, 4, SUBLANE, tq), 1) * SUBLANE \
            + lax.broadcasted_iota(jnp.int32, (kc // 32, 4, SUBLANE, tq), 2)

        def ybit_body(it, y):
            cand = y + lax.shift_left(jnp.int32(1), 11 - it)
            cnt = count(lambda st, c: (st == thr_row[None, None]) & ((kidx_t + c * kc) < cand[None, None]))
            return jnp.where(cnt < need, cand, y)

        y = lax.fori_loop(0, 12, ybit_body, jnp.zeros((SUBLANE, tq), jnp.int32))
        y_rows = to_rows(y.astype(F32))
        thr_rows = to_rows(thr_row)
        kcol = lax.broadcasted_iota(jnp.int32, (tq, kc), 1).astype(F32)

        def drop_body(c, carry):
            s = sc_s[c]
            drop = (s == thr_rows) & ((kcol + (c * kc).astype(F32)) > y_rows)
            sc_s[c] = jnp.where(drop, -jnp.inf, s)
            return carry

        lax.fori_loop(0, nchunk, drop_body, 0)

    thr = to_rows(thr_row)

    m_s[...] = jnp.full(m_s.shape, -jnp.inf, F32)

    def bias_tile(c, a, b):
        return bias_ref[jnp.clip(2 * qi + a - 2 * c - b, 0, 2)]

    def logit_body(c, carry):
        off = pl.multiple_of(c * kc, kc)
        lg = lax.dot_general(q_s[...], ckv_s[pl.ds(off, kc), :], nt_dims, preferred_element_type=F32)
        bias = jnp.concatenate(
            [jnp.concatenate([bias_tile(c, a, 0), bias_tile(c, a, 1)], axis=-1) for a in range(tq // sub)],
            axis=-2)
        keep = sc_s[c] >= thr
        lg = jnp.where(keep[None], lg.reshape(nh, tq, kc) + bias, -jnp.inf)
        lg_s[c] = lg
        m_s[...] = jnp.maximum(m_s[...], jnp.maximum(lg[..., :LANE], lg[..., LANE:]))
        return carry

    lax.fori_loop(0, nchunk, logit_body, 0)

    m = jnp.max(m_s[...], axis=-1, keepdims=True)
    m_s[...] = jnp.broadcast_to(m, m_s.shape)
    l_s[...] = jnp.zeros(l_s.shape, F32)
    acc_s[...] = jnp.zeros(acc_s.shape, F32)

    def pv_body(c, carry):
        off = pl.multiple_of(c * kc, kc)
        lg = lg_s[c]
        mb = m_s[...]
        p0 = jnp.exp2(lg[..., :LANE] - mb)
        p1 = jnp.exp2(lg[..., LANE:] - mb)
        l_s[...] += p0 + p1
        p = jnp.concatenate([p0, p1], axis=-1).reshape(nh * tq, kc).astype(BF16)
        acc_s[...] += jnp.dot(p, ckv_s[pl.ds(off, kc), :], preferred_element_type=F32)
        return carry

    lax.fori_loop(0, nchunk, pv_body, 0)

    lsum = jnp.sum(l_s[...], axis=-1, keepdims=True)
    for h in range(nh):
        o_ref[:, h * KV_RANK:(h + 1) * KV_RANK] = (acc_s[h * tq:(h + 1) * tq, :] / lsum[h]).astype(o_ref.dtype)


def _dsa(q3, p3, bias_tiles, kv_gain, topk):
    b, s, _ = p3.shape
    tq, kc, sub, rb = DSA_TQ, DSA_KC, Q_BLOCK, DSA_RB
    assert tq == kc and s % tq == 0 and s <= 2 ** 11
    nkc = s // kc
    kern = functools.partial(_dsa_kernel, topk=topk)
    return pl.pallas_call(
        kern,
        grid=(b, s // tq),
        in_specs=[
            pl.BlockSpec((None, tq, HR), lambda i, j: (i, j, 0)),
            pl.BlockSpec((None, tq, IQW), lambda i, j: (i, j, OFF_IQ // IQW)),
            pl.BlockSpec((None, s, KV_RANK), lambda i, j: (i, 0, OFF_CKV // KV_RANK)),
            pl.BlockSpec((None, s, LANE), lambda i, j: (i, 0, OFF_IKW // LANE)),
            pl.BlockSpec((None, tq, LANE), lambda i, j: (i, j, OFF_IKW // LANE)),
            pl.BlockSpec((3, ATT_HEADS, sub, sub), lambda i, j: (0, 0, 0, 0)),
            pl.BlockSpec((1, KV_RANK), lambda i, j: (0, 0)),
        ],
        out_specs=pl.BlockSpec((None, tq, HR), lambda i, j: (i, j, 0)),
        out_shape=jax.ShapeDtypeStruct((b, s, HR), BF16),
        scratch_shapes=[
            pltpu.VMEM((s, KV_RANK), BF16),
            pltpu.VMEM((s, IDX_DIM), BF16),
            pltpu.VMEM((ATT_HEADS * tq, KV_RANK), BF16),
            pltpu.VMEM((tq // rb, IDX_HEADS * rb, IDX_DIM), BF16),
            pltpu.VMEM((tq // rb, IDX_HEADS, rb, LANE), F32),
            pltpu.VMEM((nkc, tq, kc), F32),
            pltpu.VMEM((nkc, kc, tq), F32),
            pltpu.VMEM((nkc, ATT_HEADS, tq, kc), F32),
            pltpu.VMEM((ATT_HEADS, tq, LANE), F32),
            pltpu.VMEM((ATT_HEADS, tq, LANE), F32),
            pltpu.VMEM((ATT_HEADS * tq, KV_RANK), F32),
        ],
        compiler_params=_params(2),
        name="dsa",
    )(q3, p3, p3, p3, p3, bias_tiles, kv_gain)


MLSTM_CHUNK = 128


def _mlstm_kernel(mqk_ref, mv_ref, mo_ref, gt_ref, cw_ref, cb_ref, gb_ref, on_ref, o_ref,
                  c_s, n_s, m_s, ext_s):
    lc = MLSTM_CHUNK
    dk, dv = M_QK_DIM, M_V_DIM

    @pl.when(pl.program_id(1) == 0)
    def _():
        c_s[...] = jnp.zeros(c_s.shape, F32)
        n_s[...] = jnp.zeros(n_s.shape, F32)
        m_s[...] = jnp.zeros(m_s.shape, F32)
        ext_s[0:SUBLANE, :] = jnp.zeros((SUBLANE, MQK), F32)

    x = mqk_ref[...]
    ext_s[SUBLANE:SUBLANE + lc, :] = x
    conv = jnp.broadcast_to(cb_ref[...], (lc, MQK))
    for j in range(CONV_WIDTH):
        lo = SUBLANE - (CONV_WIDTH - 1) + j
        conv = conv + cw_ref[j:j + 1, :] * ext_s[lo:lo + lc, :]
    ext_s[0:SUBLANE, :] = x[lc - SUBLANE:lc, :]
    a = conv * _sigmoid(conv)

    gates = gt_ref[...] + gb_ref[...]
    lsg = _log_sigmoid(gates)
    r_i = lax.broadcasted_iota(jnp.int32, (lc, lc), 0)
    c_i = lax.broadcasted_iota(jnp.int32, (lc, lc), 1)
    tril = r_i >= c_i
    hi = lax.Precision.HIGHEST
    bcum_col = jnp.dot(tril.astype(F32), lsg, precision=hi, preferred_element_type=F32)
    gates_t = gates.T
    bcum_row = jnp.dot(lsg.T, (r_i <= c_i).astype(F32), precision=hi, preferred_element_type=F32)

    for h in range(M_HEADS):
        bc = bcum_col[:, M_HEADS + h:M_HEADS + h + 1]
        br = bcum_row[M_HEADS + h:M_HEADS + h + 1, :]
        ic = gates[:, h:h + 1]
        ir = gates_t[h:h + 1, :]
        mst = m_s[h, 0:1, 0:1]
        q = a[:, h * dk:(h + 1) * dk]
        k = a[:, MQK // 2 + h * dk:MQK // 2 + (h + 1) * dk] * (dk ** -0.5)
        v = mv_ref[:, h * dv:(h + 1) * dv]
        qb16, kb16, vb16 = q.astype(BF16), k.astype(BF16), v.astype(BF16)

        dmat = jnp.where(tril, bc - br + ir, -jnp.inf)
        inter = bc + mst
        m_row = jnp.maximum(inter, jnp.max(dmat, axis=-1, keepdims=True))
        w_inter = jnp.exp(inter - m_row)
        qk = lax.dot_general(qb16, kb16, (((1,), (1,)), ((), ())), preferred_element_type=F32)
        qk = qk * jnp.exp(dmat - m_row)
        num = (w_inter * jnp.dot(qb16, c_s[h].astype(BF16), preferred_element_type=F32)
               + jnp.dot(qk.astype(BF16), vb16, preferred_element_type=F32))
        den = (w_inter * jnp.sum(q * n_s[h, 0:1, :], axis=-1, keepdims=True)
               + jnp.sum(qk, axis=-1, keepdims=True))
        hh = num / jnp.maximum(jnp.abs(den), jnp.exp(-m_row))

        b_last = bc[lc - 1:lc, :]
        g_col = b_last - bc + ic
        g_row = b_last - br + ir
        m_new = jnp.maximum(b_last + mst, jnp.max(g_row, axis=-1, keepdims=True))
        decay = jnp.exp(b_last + mst - m_new)
        kw = jnp.exp(g_col - m_new) * k
        c_s[h] = decay * c_s[h] + lax.dot_general(kw.astype(BF16), vb16, (((0,), (0,)), ((), ())),
                                                   preferred_element_type=F32)
        n_s[h, 0:1, :] = decay * n_s[h, 0:1, :] + jnp.sum(kw, axis=0, keepdims=True)
        m_s[h] = jnp.broadcast_to(m_new, (SUBLANE, LANE))

        hn = _rms(hh, on_ref[:, h * dv:(h + 1) * dv])
        o_ref[:, h * dv:(h + 1) * dv] = (hn * _sigmoid(mo_ref[:, h * dv:(h + 1) * dv])).astype(o_ref.dtype)


def _mlstm(p3, conv_w, conv_b, gate_b, out_gain):
    b, s, _ = p3.shape
    lc = MLSTM_CHUNK
    assert s % lc == 0 and lc == LANE
    return pl.pallas_call(
        _mlstm_kernel,
        grid=(b, s // lc),
        in_specs=[
            pl.BlockSpec((None, lc, MQK), lambda i, j: (i, j, OFF_MQK // MQK)),
            pl.BlockSpec((None, lc, MV), lambda i, j: (i, j, OFF_MV // MV)),
            pl.BlockSpec((None, lc, MV), lambda i, j: (i, j, OFF_MO // MV)),
            pl.BlockSpec((None, lc, LANE), lambda i, j: (i, j, OFF_MG // LANE)),
            pl.BlockSpec((CONV_WIDTH, MQK), lambda i, j: (0, 0)),
            pl.BlockSpec((1, MQK), lambda i, j: (0, 0)),
            pl.BlockSpec((1, LANE), lambda i, j: (0, 0)),
            pl.BlockSpec((1, MV), lambda i, j: (0, 0)),
        ],
        out_specs=pl.BlockSpec((None, lc, MV), lambda i, j: (i, j, 0)),
        out_shape=jax.ShapeDtypeStruct((b, s, MV), BF16),
        scratch_shapes=[
            pltpu.VMEM((M_HEADS, M_QK_DIM, M_V_DIM), F32),
            pltpu.VMEM((M_HEADS, SUBLANE, M_QK_DIM), F32),
            pltpu.VMEM((M_HEADS, SUBLANE, LANE), F32),
            pltpu.VMEM((SUBLANE + lc, MQK), F32),
        ],
        compiler_params=_params(2),
        name="mlstm",
    )(p3, p3, p3, p3, conv_w, conv_b, gate_b, out_gain)


def _mergeout_kernel(oa_ref, hm_ref, ga_ref, gm_ref, h_ref, wa_ref, wm_ref, wo_ref, o_ref, mg_s, *, tn):
    d = o_ref.shape[1]
    oa = oa_ref[...]
    hm = hm_ref[...]
    for n in range(d // tn):
        sl = slice(n * tn, (n + 1) * tn)
        ya = jnp.dot(oa, wa_ref[:, sl], preferred_element_type=F32)
        ym = jnp.dot(hm, wm_ref[:, sl], preferred_element_type=F32)
        mg_s[:, sl] = (_sigmoid(ga_ref[:, sl]) * ya + _sigmoid(gm_ref[:, sl]) * ym).astype(BF16)
    mg = mg_s[...]
    for n in range(d // tn):
        sl = slice(n * tn, (n + 1) * tn)
        o_ref[:, sl] = h_ref[:, sl] + jnp.dot(mg, wo_ref[:, sl], preferred_element_type=F32)


def _mergeout(oa, hm, p2, h, wa, wm, wo, l):
    m, d = h.shape
    tm = _tile(m, 256)
    tn = _tile(d, 512)
    assert OFF_GA % d == 0
    ga0 = OFF_GA // d
    resident = functools.partial(pl.BlockSpec, pipeline_mode=pl.Buffered(1))
    return pl.pallas_call(
        functools.partial(_mergeout_kernel, tn=tn),
        grid=(m // tm,),
        in_specs=[
            pl.BlockSpec((tm, HR), lambda i: (i, 0)),
            pl.BlockSpec((tm, MV), lambda i: (i, 0)),
            pl.BlockSpec((tm, d), lambda i: (i, ga0)),
            pl.BlockSpec((tm, d), lambda i: (i, ga0 + 1)),
            pl.BlockSpec((tm, d), lambda i: (i, 0)),
            resident((None, HR, d), lambda i: (l, 0, 0)),
            resident((None, MV, d), lambda i: (l, 0, 0)),
            resident((None, d, d), lambda i: (l, 0, 0)),
        ],
        out_specs=pl.BlockSpec((tm, d), lambda i: (i, 0)),
        out_shape=jax.ShapeDtypeStruct((m, d), F32),
        scratch_shapes=[pltpu.VMEM((tm, d), BF16)],
        compiler_params=_params(1),
        name="mergeout",
    )(oa, hm, p2, p2, h, wa, wm, wo)


def _bias_tiles(rel_bias):
    tq = Q_BLOCK
    i = jnp.arange(tq)[:, None]
    j = jnp.arange(tq)[None, :]

    def bucket(rel):
        n = jnp.maximum(rel, 0)
        max_exact = REL_BUCKETS // 2
        n_large = jnp.maximum(n, max_exact).astype(F32)
        large = max_exact + (jnp.log(n_large / max_exact) / math.log(REL_MAX_DIST / max_exact)
                             * (REL_BUCKETS - max_exact)).astype(jnp.int32)
        large = jnp.minimum(large, REL_BUCKETS - 1)
        return jnp.where(n < max_exact, n, large)

    onehot = jax.nn.one_hot(jnp.stack([bucket(i - j + d * tq) for d in range(3)]), REL_BUCKETS, dtype=F32)
    return jnp.einsum("dijk,kh->dhij", onehot, rel_bias.astype(F32) * LOG2E, precision=lax.Precision.HIGHEST)


def _prep_w_in(w_in, d):
    widths = (HR, KV_RANK, IQW, IDX_DIM, IDX_HEADS, MQK, MV, MV, M_HEADS, M_HEADS, d, d)
    segs = []
    off = 0
    for wdt in widths:
        segs.append(w_in[..., off:off + wdt])
        off += wdt
    aq, ckv, iq, ik, iw, mqk, mv, mo, mi, mf, ga, gm = segs
    z = lambda n: jnp.zeros(w_in.shape[:-1] + (n,), w_in.dtype)
    return jnp.concatenate(
        [aq, iq, ckv, ik, iw, z(LANE - IDX_DIM - IDX_HEADS), mi, mf, z(LANE - 2 * M_HEADS), mqk, mv, mo, ga, gm],
        axis=-1)


def kernel(x, rel_bias, ffn1_norm, ffn1_w_gu, ffn1_w_down, mix_norm, w_in, q_norm, kv_norm, conv_w, conv_b,
           igate_b, fgate_b, m_out_norm, w_att_out, w_mem_out, w_out, ffn2_norm, ffn2_w_gu, ffn2_w_down):
    b, s, d = x.shape
    depth = w_in.shape[0]
    topk = min(TOPK_MAX, s // 4)
    m = b * s
    h = x.reshape(m, d)
    bias_tiles = _bias_tiles(rel_bias)
    w1gu, w1d = ffn1_w_gu.astype(BF16), ffn1_w_down.astype(BF16)
    w2gu, w2d = ffn2_w_gu.astype(BF16), ffn2_w_down.astype(BF16)
    w_in_r = _prep_w_in(w_in.astype(BF16), d)
    w_att = w_att_out.astype(BF16).reshape(depth, HR, d)
    w_mem = w_mem_out.astype(BF16)
    w_o = w_out.astype(BF16)
    gate_b = jnp.concatenate([igate_b, fgate_b, jnp.zeros((depth, LANE - 2 * M_HEADS), F32)], axis=1)
    for l in range(depth):
        h = _ffn(h, ffn1_norm[l][None], w1gu, w1d, l)

        q2, p2 = _inproj(h, mix_norm[l][None], w_in_r, q_norm[l][None], l)
        p3 = p2.reshape(b, s, p2.shape[1])
        o_att = _dsa(q2.reshape(b, s, HR), p3, bias_tiles, kv_norm[l][None], topk)
        hm = _mlstm(p3, conv_w[l], conv_b[l][None], gate_b[l][None], m_out_norm[l].reshape(1, MV))
        h = _mergeout(o_att.reshape(m, HR), hm.reshape(m, MV), p2, h, w_att, w_mem, w_o, l)

        h = _ffn(h, ffn2_norm[l][None], w2gu, w2d, l)
    return h.reshape(b, s, d)
```

```python
import functools
import math

import jax
import jax.numpy as jnp
from jax import lax
from jax.experimental import pallas as pl
from jax.experimental.pallas import tpu as pltpu

F32 = jnp.float32
BF16 = jnp.bfloat16

ATT_HEADS = 8
KV_RANK = 256
IDX_HEADS = 8
IDX_DIM = 64
TOPK_MAX = 256
Q_BLOCK = 128
REL_BUCKETS = 32
REL_MAX_DIST = 128
M_HEADS = 4
M_QK_DIM = 128
M_V_DIM = 256
CONV_WIDTH = 4
EPS = 1e-6

HR = ATT_HEADS * KV_RANK
IQW = IDX_HEADS * IDX_DIM
MQK = 2 * M_HEADS * M_QK_DIM
MV = M_HEADS * M_V_DIM
LANE = 128
SUBLANE = 8
VMEM_LIMIT = 56 * 1024 * 1024

OFF_IQ = 0
OFF_CKV = OFF_IQ + IQW
OFF_IKW = OFF_CKV + KV_RANK
OFF_MG = OFF_IKW + LANE
OFF_MQK = OFF_MG + LANE
OFF_MV = OFF_MQK + MQK
OFF_MO = OFF_MV + MV
OFF_GA = OFF_MO + MV

KEY_NEG_INF = -2139095041
LOG2E = math.log2(math.e)
Q_SCALE = KV_RANK ** -0.5 * LOG2E
F32_LOWEST = float(jnp.finfo(jnp.float32).min)


def _params(n_grid):
    return pltpu.CompilerParams(dimension_semantics=("arbitrary",) * n_grid,
                                vmem_limit_bytes=VMEM_LIMIT)


def _tile(n, pref):
    if n <= pref:
        return n
    t = (pref // LANE) * LANE
    while n % t:
        t -= LANE
    return t


def _rms(x, g):
    return x * lax.rsqrt(jnp.mean(x * x, axis=-1, keepdims=True) + EPS) * g


def _sigmoid(x):
    return 1.0 / (1.0 + jnp.exp(-x))


def _log_sigmoid(x):
    return jnp.minimum(x, 0.0) - jnp.log(1.0 + jnp.exp(-jnp.abs(x)))


def _ffn_kernel(h_ref, g_ref, wg_ref, wu_ref, wd_ref, o_ref, xn_ref, *, overlap):
    j = pl.program_id(1)

    @pl.when(j == 0)
    def _():
        x = h_ref[...]
        xn_ref[...] = _rms(x, g_ref[...]).astype(BF16)
        o_ref[...] = x

    xn = xn_ref[...]
    g = jnp.dot(xn, wg_ref[0], preferred_element_type=F32)
    u = jnp.dot(xn, wu_ref[0], preferred_element_type=F32)
    a = g * _sigmoid(g) * u * 0.5
    if overlap:
        col = lax.broadcasted_iota(jnp.int32, a.shape, 1)
        a = jnp.where((j == pl.num_programs(1) - 1) & (col < overlap), 0.0, a)
    o_ref[...] += jnp.dot(a.astype(BF16), wd_ref[0], preferred_element_type=F32)


def _ffn(h, gain, w_gu, w_down, l):
    m, d = h.shape
    f = w_down.shape[1]
    tm = _tile(m, 1024)
    tf = min(f, 512)
    nj = -(-f // tf)
    assert f % LANE == 0 and tf % LANE == 0

    def off(j):
        return pl.multiple_of(jnp.minimum(j * tf, f - tf), LANE)

    return pl.pallas_call(
        functools.partial(_ffn_kernel, overlap=nj * tf - f),
        grid=(m // tm, nj),
        in_specs=[
            pl.BlockSpec((tm, d), lambda i, j: (i, 0)),
            pl.BlockSpec((1, d), lambda i, j: (0, 0)),
            pl.BlockSpec((pl.Element(1), pl.Element(d), pl.Element(tf)), lambda i, j: (l, 0, off(j))),
            pl.BlockSpec((pl.Element(1), pl.Element(d), pl.Element(tf)),
                         lambda i, j: (l, 0, pl.multiple_of(f + off(j), LANE))),
            pl.BlockSpec((pl.Element(1), pl.Element(tf), pl.Element(d)), lambda i, j: (l, off(j), 0)),
        ],
        out_specs=pl.BlockSpec((tm, d), lambda i, j: (i, 0)),
        out_shape=jax.ShapeDtypeStruct((m, d), F32),
        scratch_shapes=[pltpu.VMEM((tm, d), BF16)],
        compiler_params=_params(2),
        name="ffn",
    )(h, gain, w_gu, w_gu, w_down)


def _inproj_kernel(h_ref, g_ref, w_ref, qn_ref, q_ref, p_ref, xn_ref, *, nq):
    j = pl.program_id(1)

    @pl.when(j == 0)
    def _():
        xn_ref[...] = _rms(h_ref[...], g_ref[...]).astype(BF16)

    @pl.when(j < nq)
    def _():
        acc = jnp.dot(xn_ref[...], w_ref[...], preferred_element_type=F32)
        for k in range(acc.shape[1] // KV_RANK):
            sl = slice(k * KV_RANK, (k + 1) * KV_RANK)
            q_ref[:, sl] = (_rms(acc[:, sl], qn_ref[...]) * Q_SCALE).astype(q_ref.dtype)

    @pl.when(j >= nq)
    def _():
        p_ref[...] = jnp.dot(xn_ref[...], w_ref[...], preferred_element_type=F32)


def _inproj(h, gain, w, q_gain, l):
    m, d = h.shape
    n = w.shape[2]
    tm = _tile(m, 1024)
    tn = 1024 if n % 1024 == 0 else 512
    assert HR % tn == 0 and n % tn == 0 and tn % KV_RANK == 0
    nq = HR // tn
    return pl.pallas_call(
        functools.partial(_inproj_kernel, nq=nq),
        grid=(m // tm, n // tn),
        in_specs=[
            pl.BlockSpec((tm, d), lambda i, j: (i, 0)),
            pl.BlockSpec((1, d), lambda i, j: (0, 0)),
            pl.BlockSpec((None, d, tn), lambda i, j: (l, 0, j)),
            pl.BlockSpec((1, KV_RANK), lambda i, j: (0, 0)),
        ],
        out_specs=[
            pl.BlockSpec((tm, tn), lambda i, j: (i, jnp.minimum(j, nq - 1))),
            pl.BlockSpec((tm, tn), lambda i, j: (i, jnp.maximum(j - nq, 0))),
        ],
        out_shape=[jax.ShapeDtypeStruct((m, HR), BF16), jax.ShapeDtypeStruct((m, n - HR), F32)],
        scratch_shapes=[pltpu.VMEM((tm, d), BF16)],
        compiler_params=_params(2),
        name="inproj",
    )(h, gain, w, q_gain)


DSA_TQ = 256
DSA_KC = 256
DSA_RB = 64


def _dsa_kernel(q_ref, iq_ref, ckv_ref, ik_ref, iw_ref, bias_ref, kn_ref, o_ref,
                ckv_s, ik_s, q_s, iq_s, wb_s, sc_s, sct_s, lg_s, m_s, l_s, acc_s, *, topk):
    qi = pl.program_id(1)
    tq, kc, nh, sub, rb = DSA_TQ, DSA_KC, ATT_HEADS, Q_BLOCK, DSA_RB
    nrb = tq // rb
    nchunk = qi + 1

    @pl.when(qi == 0)
    def _():
        ckv_s[...] = _rms(ckv_ref[...], kn_ref[...]).astype(BF16)
        ik_s[...] = ik_ref[:, :IDX_DIM].astype(BF16)

    for h in range(nh):
        q_s[h * tq:(h + 1) * tq, :] = q_ref[:, h * KV_RANK:(h + 1) * KV_RANK]
    w = iw_ref[:, IDX_DIM:IDX_DIM + IDX_HEADS] * (IQW ** -0.5)
    for r in range(nrb):
        for h in range(IDX_HEADS):
            iq_s[r, h * rb:(h + 1) * rb, :] = iq_ref[r * rb:(r + 1) * rb, h * IDX_DIM:(h + 1) * IDX_DIM].astype(BF16)
            wb_s[r, h] = jnp.broadcast_to(w[r * rb:(r + 1) * rb, h:h + 1], (rb, LANE))

    nt_dims = (((1,), (1,)), ((), ()))
    row_rb = lax.broadcasted_iota(jnp.int32, (rb, kc), 0)
    col_rb = lax.broadcasted_iota(jnp.int32, (rb, kc), 1)

    last = nchunk - 1

    def for_chunk_pairs(body, start):
        def pair(p, carry):
            c0 = start + 2 * p
            body(c0)
            body(jnp.minimum(c0 + 1, last))
            return carry

        lax.fori_loop(0, (nchunk - start + 1) // 2, pair, 0)

    def score_chunk(c):
        off = pl.multiple_of(c * kc, kc)
        ikc = ik_s[pl.ds(off, kc), :]
        for r in range(nrb):
            d = lax.dot_general(iq_s[r], ikc, nt_dims, preferred_element_type=F32)
            d = jnp.maximum(d, 0.0).reshape(IDX_HEADS, rb, kc)
            wb = wb_s[r]
            s = jnp.sum(d * jnp.concatenate([wb, wb], axis=-1), axis=0)
            causal = (c * kc + col_rb) <= (qi * tq + r * rb + row_rb)
            sc_s[c, r * rb:(r + 1) * rb, :] = jnp.where(causal, s, -jnp.inf)
        sct_s[c] = sc_s[c].T

    for_chunk_pairs(score_chunk, 0)

    kf = float(topk)

    def count(pred):
        def chunk_body(c, acc):
            hit = jnp.where(pred(sct_s[c].reshape(kc // 32, 4, SUBLANE, tq)), 1.0, 0.0)
            return acc + jnp.sum(jnp.sum(hit, axis=1), axis=0)

        acc = lax.fori_loop(0, nchunk, chunk_body, jnp.zeros((SUBLANE, tq), F32))
        return jnp.sum(acc, axis=0, keepdims=True)

    def key_to_float(key):
        return lax.bitcast_convert_type(jnp.where(key >= 0, key, key ^ jnp.int32(0x7FFFFFFF)), F32)

    def bit_body(it, t_key):
        cand = t_key + lax.shift_left(jnp.int32(1), 31 - it)
        cf = key_to_float(cand)
        cnt = count(lambda st: st >= cf[None, None])
        ok = (cnt >= kf) | (cand < KEY_NEG_INF)
        return jnp.where(ok, cand, t_key)

    t_key = lax.fori_loop(0, 32, bit_body, jnp.full((SUBLANE, tq), -2 ** 31, jnp.int32))
    thr_row = jnp.maximum(key_to_float(t_key), F32_LOWEST)

    def to_rows(v):
        return jnp.broadcast_to(v[0:1, :], (kc, tq)).T

    n_gt = count(lambda st: st > thr_row[None, None])
    n_ge = count(lambda st: st >= thr_row[None, None])
    need = kf - n_gt

    @pl.when(jnp.max(n_ge - kf) > 0.0)
    def _():
        kidx_t = lax.broadcasted_iota(jnp.int32, (kc, tq), 0).astype(F32)

        def mark_body(c, carry):
            sct_s[c] = jnp.where(sct_s[c] == thr_row[0:1, :], kidx_t + (c * kc).astype(F32), jnp.inf)
            return carry

        lax.fori_loop(0, nchunk, mark_body, 0)

        def ybit_body(it, y):
            cand = y + lax.shift_left(jnp.int32(1), 11 - it)
            cnt = count(lambda tk: tk < cand.astype(F32)[None, None])
            return jnp.where(cnt < need, cand, y)

        y = lax.fori_loop(0, 12, ybit_body, jnp.zeros((SUBLANE, tq), jnp.int32))
        y_rows = to_rows(y.astype(F32))
        thr_rows = to_rows(thr_row)
        kcol = lax.broadcasted_iota(jnp.int32, (tq, kc), 1).astype(F32)

        def drop_body(c, carry):
            s = sc_s[c]
            drop = (s == thr_rows) & ((kcol + (c * kc).astype(F32)) > y_rows)
            sc_s[c] = jnp.where(drop, -jnp.inf, s)
            return carry

        lax.fori_loop(0, nchunk, drop_body, 0)

    thr = to_rows(thr_row)


    def bias_tile(c, a, b):
        return bias_ref[jnp.clip(2 * qi + a - 2 * c - b, 0, 2)]

    def logit_chunk(c, first=False):
        off = pl.multiple_of(c * kc, kc)
        lg = lax.dot_general(q_s[...], ckv_s[pl.ds(off, kc), :], nt_dims, preferred_element_type=F32)
        bias = jnp.concatenate(
            [jnp.concatenate([bias_tile(c, a, 0), bias_tile(c, a, 1)], axis=-1) for a in range(tq // sub)],
            axis=-2)
        keep = sc_s[c] >= thr
        lg = jnp.where(keep[None], lg.reshape(nh, tq, kc) + bias, -jnp.inf)
        lg_s[c] = lg
        mx = jnp.maximum(lg[..., :LANE], lg[..., LANE:])
        m_s[...] = mx if first else jnp.maximum(m_s[...], mx)

    logit_chunk(0, first=True)
    for_chunk_pairs(logit_chunk, 1)

    m = jnp.max(m_s[...], axis=-1, keepdims=True)
    m_s[...] = jnp.broadcast_to(m, m_s.shape)

    def pv_chunk(c, first=False):
        off = pl.multiple_of(c * kc, kc)
        lg = lg_s[c]
        mb = m_s[...]
        p0 = jnp.exp2(lg[..., :LANE] - mb)
        p1 = jnp.exp2(lg[..., LANE:] - mb)
        p = jnp.concatenate([p0, p1], axis=-1).reshape(nh * tq, kc).astype(BF16)
        pv = jnp.dot(p, ckv_s[pl.ds(off, kc), :], preferred_element_type=F32)
        if first:
            l_s[...] = p0 + p1
            acc_s[...] = pv
        else:
            l_s[...] += p0 + p1
            acc_s[...] += pv

    pv_chunk(0, first=True)

    def pv_pair(p, carry):
        pv_chunk(1 + 2 * p)
        pv_chunk(2 + 2 * p)
        return carry

    lax.fori_loop(0, (nchunk - 1) // 2, pv_pair, 0)

    @pl.when(nchunk % 2 == 0)
    def _():
        pv_chunk(last)

    lsum = jnp.sum(l_s[...], axis=-1, keepdims=True)
    for h in range(nh):
        o_ref[:, h * KV_RANK:(h + 1) * KV_RANK] = (acc_s[h * tq:(h + 1) * tq, :] / lsum[h]).astype(o_ref.dtype)


def _dsa(q3, p3, bias_tiles, kv_gain, topk):
    b, s, _ = p3.shape
    tq, kc, sub, rb = DSA_TQ, DSA_KC, Q_BLOCK, DSA_RB
    assert tq == kc and s % tq == 0 and s <= 2 ** 11
    nkc = s // kc
    kern = functools.partial(_dsa_kernel, topk=topk)
    return pl.pallas_call(
        kern,
        grid=(b, s // tq),
        in_specs=[
            pl.BlockSpec((None, tq, HR), lambda i, j: (i, j, 0)),
            pl.BlockSpec((None, tq, IQW), lambda i, j: (i, j, OFF_IQ // IQW)),
            pl.BlockSpec((None, s, KV_RANK), lambda i, j: (i, 0, OFF_CKV // KV_RANK)),
            pl.BlockSpec((None, s, LANE), lambda i, j: (i, 0, OFF_IKW // LANE)),
            pl.BlockSpec((None, tq, LANE), lambda i, j: (i, j, OFF_IKW // LANE)),
            pl.BlockSpec((3, ATT_HEADS, sub, sub), lambda i, j: (0, 0, 0, 0)),
            pl.BlockSpec((1, KV_RANK), lambda i, j: (0, 0)),
        ],
        out_specs=pl.BlockSpec((None, tq, HR), lambda i, j: (i, j, 0)),
        out_shape=jax.ShapeDtypeStruct((b, s, HR), BF16),
        scratch_shapes=[
            pltpu.VMEM((s, KV_RANK), BF16),
            pltpu.VMEM((s, IDX_DIM), BF16),
            pltpu.VMEM((ATT_HEADS * tq, KV_RANK), BF16),
            pltpu.VMEM((tq // rb, IDX_HEADS * rb, IDX_DIM), BF16),
            pltpu.VMEM((tq // rb, IDX_HEADS, rb, LANE), F32),
            pltpu.VMEM((nkc, tq, kc), F32),
            pltpu.VMEM((nkc, kc, tq), F32),
            pltpu.VMEM((nkc, ATT_HEADS, tq, kc), F32),
            pltpu.VMEM((ATT_HEADS, tq, LANE), F32),
            pltpu.VMEM((ATT_HEADS, tq, LANE), F32),
            pltpu.VMEM((ATT_HEADS * tq, KV_RANK), F32),
        ],
        compiler_params=_params(2),
        name="dsa",
    )(q3, p3, p3, p3, p3, bias_tiles, kv_gain)


MLSTM_CHUNK = 128


def _mlstm_kernel(mqk_ref, mv_ref, mo_ref, gt_ref, cw_ref, cb_ref, gb_ref, on_ref, o_ref,
                  c_s, n_s, m_s, ext_s):
    lc = MLSTM_CHUNK
    dk, dv = M_QK_DIM, M_V_DIM

    @pl.when(pl.program_id(1) == 0)
    def _():
        c_s[...] = jnp.zeros(c_s.shape, F32)
        n_s[...] = jnp.zeros(n_s.shape, F32)
        m_s[...] = jnp.zeros(m_s.shape, F32)
        ext_s[0:SUBLANE, :] = jnp.zeros((SUBLANE, MQK), F32)

    x = mqk_ref[...]
    ext_s[SUBLANE:SUBLANE + lc, :] = x
    conv = jnp.broadcast_to(cb_ref[...], (lc, MQK))
    for j in range(CONV_WIDTH):
        lo = SUBLANE - (CONV_WIDTH - 1) + j
        conv = conv + cw_ref[j:j + 1, :] * ext_s[lo:lo + lc, :]
    ext_s[0:SUBLANE, :] = x[lc - SUBLANE:lc, :]
    a = conv * _sigmoid(conv)

    gates = gt_ref[...] + gb_ref[...]
    lsg = _log_sigmoid(gates)
    r_i = lax.broadcasted_iota(jnp.int32, (lc, lc), 0)
    c_i = lax.broadcasted_iota(jnp.int32, (lc, lc), 1)
    tril = r_i >= c_i
    hi = lax.Precision.HIGHEST
    bcum_col = jnp.dot(tril.astype(F32), lsg, precision=hi, preferred_element_type=F32)
    gates_t = gates.T
    bcum_row = jnp.dot(lsg.T, (r_i <= c_i).astype(F32), precision=hi, preferred_element_type=F32)

    for h in range(M_HEADS):
        bc = bcum_col[:, M_HEADS + h:M_HEADS + h + 1]
        br = bcum_row[M_HEADS + h:M_HEADS + h + 1, :]
        ic = gates[:, h:h + 1]
        ir = gates_t[h:h + 1, :]
        mst = m_s[h, 0:1, 0:1]
        q = a[:, h * dk:(h + 1) * dk]
        k = a[:, MQK // 2 + h * dk:MQK // 2 + (h + 1) * dk] * (dk ** -0.5)
        v = mv_ref[:, h * dv:(h + 1) * dv]
        qb16, kb16, vb16 = q.astype(BF16), k.astype(BF16), v.astype(BF16)

        dmat = jnp.where(tril, bc - br + ir, -jnp.inf)
        inter = bc + mst
        m_row = jnp.maximum(inter, jnp.max(dmat, axis=-1, keepdims=True))
        w_inter = jnp.exp(inter - m_row)
        qk = lax.dot_general(qb16, kb16, (((1,), (1,)), ((), ())), preferred_element_type=F32)
        qk = qk * jnp.exp(dmat - m_row)
        num = (w_inter * jnp.dot(qb16, c_s[h].astype(BF16), preferred_element_type=F32)
               + jnp.dot(qk.astype(BF16), vb16, preferred_element_type=F32))
        den = (w_inter * jnp.sum(q * n_s[h, 0:1, :], axis=-1, keepdims=True)
               + jnp.sum(qk, axis=-1, keepdims=True))
        hh = num / jnp.maximum(jnp.abs(den), jnp.exp(-m_row))

        b_last = bc[lc - 1:lc, :]
        g_col = b_last - bc + ic
        g_row = b_last - br + ir
        m_new = jnp.maximum(b_last + mst, jnp.max(g_row, axis=-1, keepdims=True))
        decay = jnp.exp(b_last + mst - m_new)
        kw = jnp.exp(g_col - m_new) * k
        c_s[h] = decay * c_s[h] + lax.dot_general(kw.astype(BF16), vb16, (((0,), (0,)), ((), ())),
                                                   preferred_element_type=F32)
        n_s[h, 0:1, :] = decay * n_s[h, 0:1, :] + jnp.sum(kw, axis=0, keepdims=True)
        m_s[h] = jnp.broadcast_to(m_new, (SUBLANE, LANE))

        hn = _rms(hh, on_ref[:, h * dv:(h + 1) * dv])
        o_ref[:, h * dv:(h + 1) * dv] = (hn * _sigmoid(mo_ref[:, h * dv:(h + 1) * dv])).astype(o_ref.dtype)


def _mlstm(p3, conv_w, conv_b, gate_b, out_gain):
    b, s, _ = p3.shape
    lc = MLSTM_CHUNK
    assert s % lc == 0 and lc == LANE
    return pl.pallas_call(
        _mlstm_kernel,
        grid=(b, s // lc),
        in_specs=[
            pl.BlockSpec((None, lc, MQK), lambda i, j: (i, j, OFF_MQK // MQK)),
            pl.BlockSpec((None, lc, MV), lambda i, j: (i, j, OFF_MV // MV)),
            pl.BlockSpec((None, lc, MV), lambda i, j: (i, j, OFF_MO // MV)),
            pl.BlockSpec((None, lc, LANE), lambda i, j: (i, j, OFF_MG // LANE)),
            pl.BlockSpec((CONV_WIDTH, MQK), lambda i, j: (0, 0)),
            pl.BlockSpec((1, MQK), lambda i, j: (0, 0)),
            pl.BlockSpec((1, LANE), lambda i, j: (0, 0)),
            pl.BlockSpec((1, MV), lambda i, j: (0, 0)),
        ],
        out_specs=pl.BlockSpec((None, lc, MV), lambda i, j: (i, j, 0)),
        out_shape=jax.ShapeDtypeStruct((b, s, MV), BF16),
        scratch_shapes=[
            pltpu.VMEM((M_HEADS, M_QK_DIM, M_V_DIM), F32),
            pltpu.VMEM((M_HEADS, SUBLANE, M_QK_DIM), F32),
            pltpu.VMEM((M_HEADS, SUBLANE, LANE), F32),
            pltpu.VMEM((SUBLANE + lc, MQK), F32),
        ],
        compiler_params=_params(2),
        name="mlstm",
    )(p3, p3, p3, p3, conv_w, conv_b, gate_b, out_gain)


def _mergeout_kernel(oa_ref, hm_ref, ga_ref, gm_ref, h_ref, wa_ref, wm_ref, wo_ref, o_ref, mg_s, *, tn):
    d = o_ref.shape[1]
    oa = oa_ref[...]
    hm = hm_ref[...]
    for n in range(d // tn):
        sl = slice(n * tn, (n + 1) * tn)
        ya = jnp.dot(oa, wa_ref[:, sl], preferred_element_type=F32)
        ym = jnp.dot(hm, wm_ref[:, sl], preferred_element_type=F32)
        mg_s[:, sl] = (_sigmoid(ga_ref[:, sl]) * ya + _sigmoid(gm_ref[:, sl]) * ym).astype(BF16)
    mg = mg_s[...]
    for n in range(d // tn):
        sl = slice(n * tn, (n + 1) * tn)
        o_ref[:, sl] = h_ref[:, sl] + jnp.dot(mg, wo_ref[:, sl], preferred_element_type=F32)


def _mergeout(oa, hm, p2, h, wa, wm, wo, l):
    m, d = h.shape
    tm = _tile(m, 256)
    tn = _tile(d, 512)
    assert OFF_GA % d == 0
    ga0 = OFF_GA // d
    resident = functools.partial(pl.BlockSpec, pipeline_mode=pl.Buffered(1))
    return pl.pallas_call(
        functools.partial(_mergeout_kernel, tn=tn),
        grid=(m // tm,),
        in_specs=[
            pl.BlockSpec((tm, HR), lambda i: (i, 0)),
            pl.BlockSpec((tm, MV), lambda i: (i, 0)),
            pl.BlockSpec((tm, d), lambda i: (i, ga0)),
            pl.BlockSpec((tm, d), lambda i: (i, ga0 + 1)),
            pl.BlockSpec((tm, d), lambda i: (i, 0)),
            resident((None, HR, d), lambda i: (l, 0, 0)),
            resident((None, MV, d), lambda i: (l, 0, 0)),
            resident((None, d, d), lambda i: (l, 0, 0)),
        ],
        out_specs=pl.BlockSpec((tm, d), lambda i: (i, 0)),
        out_shape=jax.ShapeDtypeStruct((m, d), F32),
        scratch_shapes=[pltpu.VMEM((tm, d), BF16)],
        compiler_params=_params(1),
        name="mergeout",
    )(oa, hm, p2, p2, h, wa, wm, wo)


def _bias_tiles(rel_bias):
    tq = Q_BLOCK
    i = jnp.arange(tq)[:, None]
    j = jnp.arange(tq)[None, :]

    def bucket(rel):
        n = jnp.maximum(rel, 0)
        max_exact = REL_BUCKETS // 2
        n_large = jnp.maximum(n, max_exact).astype(F32)
        large = max_exact + (jnp.log(n_large / max_exact) / math.log(REL_MAX_DIST / max_exact)
                             * (REL_BUCKETS - max_exact)).astype(jnp.int32)
        large = jnp.minimum(large, REL_BUCKETS - 1)
        return jnp.where(n < max_exact, n, large)

    onehot = jax.nn.one_hot(jnp.stack([bucket(i - j + d * tq) for d in range(3)]), REL_BUCKETS, dtype=F32)
    return jnp.einsum("dijk,kh->dhij", onehot, rel_bias.astype(F32) * LOG2E, precision=lax.Precision.HIGHEST)


def _prep_w_in(w_in, d):
    widths = (HR, KV_RANK, IQW, IDX_DIM, IDX_HEADS, MQK, MV, MV, M_HEADS, M_HEADS, d, d)
    segs = []
    off = 0
    for wdt in widths:
        segs.append(w_in[..., off:off + wdt])
        off += wdt
    aq, ckv, iq, ik, iw, mqk, mv, mo, mi, mf, ga, gm = segs
    z = lambda n: jnp.zeros(w_in.shape[:-1] + (n,), w_in.dtype)
    return jnp.concatenate(
        [aq, iq, ckv, ik, iw, z(LANE - IDX_DIM - IDX_HEADS), mi, mf, z(LANE - 2 * M_HEADS), mqk, mv, mo, ga, gm],
        axis=-1)


def kernel(x, rel_bias, ffn1_norm, ffn1_w_gu, ffn1_w_down, mix_norm, w_in, q_norm, kv_norm, conv_w, conv_b,
           igate_b, fgate_b, m_out_norm, w_att_out, w_mem_out, w_out, ffn2_norm, ffn2_w_gu, ffn2_w_down):
    b, s, d = x.shape
    depth = w_in.shape[0]
    topk = min(TOPK_MAX, s // 4)
    m = b * s
    h = x.reshape(m, d)
    bias_tiles = _bias_tiles(rel_bias)
    w1gu, w1d = ffn1_w_gu.astype(BF16), ffn1_w_down.astype(BF16)
    w2gu, w2d = ffn2_w_gu.astype(BF16), ffn2_w_down.astype(BF16)
    w_in_r = _prep_w_in(w_in.astype(BF16), d)
    w_att = w_att_out.astype(BF16).reshape(depth, HR, d)
    w_mem = w_mem_out.astype(BF16)
    w_o = w_out.astype(BF16)
    gate_b = jnp.concatenate([igate_b, fgate_b, jnp.zeros((depth, LANE - 2 * M_HEADS), F32)], axis=1)
    for l in range(depth):
        h = _ffn(h, ffn1_norm[l][None], w1gu, w1d, l)

        q2, p2 = _inproj(h, mix_norm[l][None], w_in_r, q_norm[l][None], l)
        p3 = p2.reshape(b, s, p2.shape[1])
        o_att = _dsa(q2.reshape(b, s, HR), p3, bias_tiles, kv_norm[l][None], topk)
        hm = _mlstm(p3, conv_w[l], conv_b[l][None], gate_b[l][None], m_out_norm[l].reshape(1, MV))
        h = _mergeout(o_att.reshape(m, HR), hm.reshape(m, MV), p2, h, w_att, w_mem, w_o, l)

        h = _ffn(h, ffn2_norm[l][None], w2gu, w2d, l)
    return h.reshape(b, s, d)
```

```python
import functools
import math

import jax
import jax.numpy as jnp
from jax import lax
from jax.experimental import pallas as pl
from jax.experimental.pallas import tpu as pltpu

F32 = jnp.float32
BF16 = jnp.bfloat16

ATT_HEADS = 8
KV_RANK = 256
IDX_HEADS = 8
IDX_DIM = 64
TOPK_MAX = 256
Q_BLOCK = 128
REL_BUCKETS = 32
REL_MAX_DIST = 128
M_HEADS = 4
M_QK_DIM = 128
M_V_DIM = 256
CONV_WIDTH = 4
EPS = 1e-6

HR = ATT_HEADS * KV_RANK
IQW = IDX_HEADS * IDX_DIM
MQK = 2 * M_HEADS * M_QK_DIM
MV = M_HEADS * M_V_DIM
LANE = 128
SUBLANE = 8
VMEM_LIMIT = 56 * 1024 * 1024

OFF_IQ = 0
OFF_CKV = OFF_IQ + IQW
OFF_IKW = OFF_CKV + KV_RANK
OFF_MG = OFF_IKW + LANE
OFF_MQK = OFF_MG + LANE
OFF_MV = OFF_MQK + MQK
OFF_MO = OFF_MV + MV
OFF_GA = OFF_MO + MV

KEY_NEG_INF = -2139095041
LOG2E = math.log2(math.e)
Q_SCALE = KV_RANK ** -0.5 * LOG2E
F32_LOWEST = float(jnp.finfo(jnp.float32).min)


def _params(n_grid):
    return pltpu.CompilerParams(dimension_semantics=("arbitrary",) * n_grid,
                                vmem_limit_bytes=VMEM_LIMIT)


def _tile(n, pref):
    if n <= pref:
        return n
    t = (pref // LANE) * LANE
    while n % t:
        t -= LANE
    return t


def _rms(x, g):
    return x * lax.rsqrt(jnp.mean(x * x, axis=-1, keepdims=True) + EPS) * g


def _sigmoid(x):
    return 1.0 / (1.0 + jnp.exp(-x))


def _log_sigmoid(x):
    return jnp.minimum(x, 0.0) - jnp.log(1.0 + jnp.exp(-jnp.abs(x)))


def _ffn_kernel(h_ref, g_ref, wg_ref, wu_ref, wd_ref, o_ref, xn_ref, *, overlap):
    j = pl.program_id(1)

    @pl.when(j == 0)
    def _():
        x = h_ref[...]
        xn_ref[...] = _rms(x, g_ref[...]).astype(BF16)
        o_ref[...] = x

    xn = xn_ref[...]
    g = jnp.dot(xn, wg_ref[0], preferred_element_type=F32)
    u = jnp.dot(xn, wu_ref[0], preferred_element_type=F32)
    a = g * _sigmoid(g) * u * 0.5
    if overlap:
        col = lax.broadcasted_iota(jnp.int32, a.shape, 1)
        a = jnp.where((j == pl.num_programs(1) - 1) & (col < overlap), 0.0, a)
    o_ref[...] += jnp.dot(a.astype(BF16), wd_ref[0], preferred_element_type=F32)


def _ffn(h, gain, w_gu, w_down, l):
    m, d = h.shape
    f = w_down.shape[1]
    tm = _tile(m, 1024)
    tf = min(f, 512)
    nj = -(-f // tf)
    assert f % LANE == 0 and tf % LANE == 0

    def off(j):
        return pl.multiple_of(jnp.minimum(j * tf, f - tf), LANE)

    return pl.pallas_call(
        functools.partial(_ffn_kernel, overlap=nj * tf - f),
        grid=(m // tm, nj),
        in_specs=[
            pl.BlockSpec((tm, d), lambda i, j: (i, 0)),
            pl.BlockSpec((1, d), lambda i, j: (0, 0)),
            pl.BlockSpec((pl.Element(1), pl.Element(d), pl.Element(tf)), lambda i, j: (l, 0, off(j))),
            pl.BlockSpec((pl.Element(1), pl.Element(d), pl.Element(tf)),
                         lambda i, j: (l, 0, pl.multiple_of(f + off(j), LANE))),
            pl.BlockSpec((pl.Element(1), pl.Element(tf), pl.Element(d)), lambda i, j: (l, off(j), 0)),
        ],
        out_specs=pl.BlockSpec((tm, d), lambda i, j: (i, 0)),
        out_shape=jax.ShapeDtypeStruct((m, d), F32),
        scratch_shapes=[pltpu.VMEM((tm, d), BF16)],
        compiler_params=_params(2),
        name="ffn",
    )(h, gain, w_gu, w_gu, w_down)


def _inproj_kernel(h_ref, g_ref, w_ref, qn_ref, q_ref, p_ref, xn_ref, *, nq):
    j = pl.program_id(1)

    @pl.when(j == 0)
    def _():
        xn_ref[...] = _rms(h_ref[...], g_ref[...]).astype(BF16)

    @pl.when(j < nq)
    def _():
        acc = jnp.dot(xn_ref[...], w_ref[...], preferred_element_type=F32)
        for k in range(acc.shape[1] // KV_RANK):
            sl = slice(k * KV_RANK, (k + 1) * KV_RANK)
            q_ref[:, sl] = (_rms(acc[:, sl], qn_ref[...]) * Q_SCALE).astype(q_ref.dtype)

    @pl.when(j >= nq)
    def _():
        p_ref[...] = jnp.dot(xn_ref[...], w_ref[...], preferred_element_type=F32)


def _inproj(h, gain, w, q_gain, l):
    m, d = h.shape
    n = w.shape[2]
    tm = _tile(m, 1024)
    tn = 1024 if n % 1024 == 0 else 512
    assert HR % tn == 0 and n % tn == 0 and tn % KV_RANK == 0
    nq = HR // tn
    return pl.pallas_call(
        functools.partial(_inproj_kernel, nq=nq),
        grid=(m // tm, n // tn),
        in_specs=[
            pl.BlockSpec((tm, d), lambda i, j: (i, 0)),
            pl.BlockSpec((1, d), lambda i, j: (0, 0)),
            pl.BlockSpec((None, d, tn), lambda i, j: (l, 0, j)),
            pl.BlockSpec((1, KV_RANK), lambda i, j: (0, 0)),
        ],
        out_specs=[
            pl.BlockSpec((tm, tn), lambda i, j: (i, jnp.minimum(j, nq - 1))),
            pl.BlockSpec((tm, tn), lambda i, j: (i, jnp.maximum(j - nq, 0))),
        ],
        out_shape=[jax.ShapeDtypeStruct((m, HR), BF16), jax.ShapeDtypeStruct((m, n - HR), F32)],
        scratch_shapes=[pltpu.VMEM((tm, d), BF16)],
        compiler_params=_params(2),
        name="inproj",
    )(h, gain, w, q_gain)


DSA_TQ = 256
DSA_KC = 256
DSA_RB = 64


def _dsa_kernel(q_ref, iq_ref, ckv_ref, ik_ref, iw_ref, bias_ref, kn_ref, o_ref,
                ckv_s, ik_s, q_s, iq_s, wb_s, sc_s, sct_s, lg_s, m_s, l_s, acc_s, *, topk):
    qi = pl.program_id(1)
    tq, kc, nh, sub, rb = DSA_TQ, DSA_KC, ATT_HEADS, Q_BLOCK, DSA_RB
    nrb = tq // rb
    nchunk = qi + 1

    @pl.when(qi == 0)
    def _():
        ckv_s[...] = _rms(ckv_ref[...], kn_ref[...]).astype(BF16)
        ik_s[...] = ik_ref[:, :IDX_DIM].astype(BF16)

    for h in range(nh):
        q_s[h * tq:(h + 1) * tq, :] = q_ref[:, h * KV_RANK:(h + 1) * KV_RANK]
    w = iw_ref[:, IDX_DIM:IDX_DIM + IDX_HEADS] * (IQW ** -0.5)
    for r in range(nrb):
        for h in range(IDX_HEADS):
            iq_s[r, h * rb:(h + 1) * rb, :] = iq_ref[r * rb:(r + 1) * rb, h * IDX_DIM:(h + 1) * IDX_DIM].astype(BF16)
            wb_s[r, h] = jnp.broadcast_to(w[r * rb:(r + 1) * rb, h:h + 1], (rb, LANE))

    nt_dims = (((1,), (1,)), ((), ()))
    row_rb = lax.broadcasted_iota(jnp.int32, (rb, kc), 0)
    col_rb = lax.broadcasted_iota(jnp.int32, (rb, kc), 1)

    last = nchunk - 1

    def for_chunk_pairs(body, start):
        def pair(p, carry):
            c0 = start + 2 * p
            body(c0)
            body(jnp.minimum(c0 + 1, last))
            return carry

        lax.fori_loop(0, (nchunk - start + 1) // 2, pair, 0)

    def score_chunk(c):
        off = pl.multiple_of(c * kc, kc)
        ikc = ik_s[pl.ds(off, kc), :]
        for r in range(nrb):
            d = lax.dot_general(iq_s[r], ikc, nt_dims, preferred_element_type=F32)
            d = jnp.maximum(d, 0.0).reshape(IDX_HEADS, rb, kc)
            wb = wb_s[r]
            s = jnp.sum(d * jnp.concatenate([wb, wb], axis=-1), axis=0)
            causal = (c * kc + col_rb) <= (qi * tq + r * rb + row_rb)
            sc_s[c, r * rb:(r + 1) * rb, :] = jnp.where(causal, s, -jnp.inf)
        sct_s[c] = sc_s[c].T

    for_chunk_pairs(score_chunk, 0)

    npair = (nchunk + 1) // 2

    @pl.when(nchunk % 2 == 1)
    def _():
        sct_s[nchunk] = jnp.full((kc, tq), -jnp.inf, F32)

    kf = float(topk)

    def count(pred):
        def pair_body(p, acc):
            for c in (2 * p, 2 * p + 1):
                hit = jnp.where(pred(sct_s[c].reshape(kc // 32, 4, SUBLANE, tq)), 1.0, 0.0)
                acc = acc + jnp.sum(jnp.sum(hit, axis=1), axis=0)
            return acc

        acc = lax.fori_loop(0, npair, pair_body, jnp.zeros((SUBLANE, tq), F32))
        return jnp.sum(acc, axis=0, keepdims=True)

    def key_to_float(key):
        return lax.bitcast_convert_type(jnp.where(key >= 0, key, key ^ jnp.int32(0x7FFFFFFF)), F32)

    def bit_body(it, t_key):
        cand = t_key + lax.shift_left(jnp.int32(1), 31 - it)
        cf = key_to_float(cand)
        cnt = count(lambda st: st >= cf[None, None])
        ok = (cnt >= kf) | (cand < KEY_NEG_INF)
        return jnp.where(ok, cand, t_key)

    t_key = lax.fori_loop(0, 32, bit_body, jnp.full((SUBLANE, tq), -2 ** 31, jnp.int32))
    thr_row = jnp.maximum(key_to_float(t_key), F32_LOWEST)

    def to_rows(v):
        return jnp.broadcast_to(v[0:1, :], (kc, tq)).T

    n_gt = count(lambda st: st > thr_row[None, None])
    n_ge = count(lambda st: st >= thr_row[None, None])
    need = kf - n_gt

    @pl.when(jnp.max(n_ge - kf) > 0.0)
    def _():
        kidx_t = lax.broadcasted_iota(jnp.int32, (kc, tq), 0).astype(F32)

        def mark_body(c, carry):
            sct_s[c] = jnp.where(sct_s[c] == thr_row[0:1, :], kidx_t + (c * kc).astype(F32), jnp.inf)
            return carry

        lax.fori_loop(0, 2 * npair, mark_body, 0)

        def ybit_body(it, y):
            cand = y + lax.shift_left(jnp.int32(1), 11 - it)
            cnt = count(lambda tk: tk < cand.astype(F32)[None, None])
            return jnp.where(cnt < need, cand, y)

        y = lax.fori_loop(0, 12, ybit_body, jnp.zeros((SUBLANE, tq), jnp.int32))
        y_rows = to_rows(y.astype(F32))
        thr_rows = to_rows(thr_row)
        kcol = lax.broadcasted_iota(jnp.int32, (tq, kc), 1).astype(F32)

        def drop_body(c, carry):
            s = sc_s[c]
            drop = (s == thr_rows) & ((kcol + (c * kc).astype(F32)) > y_rows)
            sc_s[c] = jnp.where(drop, -jnp.inf, s)
            return carry

        lax.fori_loop(0, nchunk, drop_body, 0)

    thr = to_rows(thr_row)


    def bias_tile(c, a, b):
        return bias_ref[jnp.clip(2 * qi + a - 2 * c - b, 0, 2)]

    def logit_chunk(c, first=False):
        off = pl.multiple_of(c * kc, kc)
        lg = lax.dot_general(q_s[...], ckv_s[pl.ds(off, kc), :], nt_dims, preferred_element_type=F32)
        bias = jnp.concatenate(
            [jnp.concatenate([bias_tile(c, a, 0), bias_tile(c, a, 1)], axis=-1) for a in range(tq // sub)],
            axis=-2)
        keep = sc_s[c] >= thr
        lg = jnp.where(keep[None], lg.reshape(nh, tq, kc) + bias, -jnp.inf)
        lg_s[c] = lg
        mx = jnp.maximum(lg[..., :LANE], lg[..., LANE:])
        m_s[...] = mx if first else jnp.maximum(m_s[...], mx)

    logit_chunk(0, first=True)
    for_chunk_pairs(logit_chunk, 1)

    m = jnp.max(m_s[...], axis=-1, keepdims=True)
    m_s[...] = jnp.broadcast_to(m, m_s.shape)

    def pv_chunk(c, first=False):
        off = pl.multiple_of(c * kc, kc)
        lg = lg_s[c]
        mb = m_s[...]
        p0 = jnp.exp2(lg[..., :LANE] - mb)
        p1 = jnp.exp2(lg[..., LANE:] - mb)
        p = jnp.concatenate([p0, p1], axis=-1).reshape(nh * tq, kc).astype(BF16)
        pv = jnp.dot(p, ckv_s[pl.ds(off, kc), :], preferred_element_type=F32)
        if first:
            l_s[...] = p0 + p1
            acc_s[...] = pv
        else:
            l_s[...] += p0 + p1
            acc_s[...] += pv

    pv_chunk(0, first=True)

    def pv_pair(p, carry):
        pv_chunk(1 + 2 * p)
        pv_chunk(2 + 2 * p)
        return carry

    lax.fori_loop(0, (nchunk - 1) // 2, pv_pair, 0)

    @pl.when(nchunk % 2 == 0)
    def _():
        pv_chunk(last)

    lsum = jnp.sum(l_s[...], axis=-1, keepdims=True)
    for h in range(nh):
        o_ref[:, h * KV_RANK:(h + 1) * KV_RANK] = (acc_s[h * tq:(h + 1) * tq, :] / lsum[h]).astype(o_ref.dtype)


def _dsa(q3, p3, bias_tiles, kv_gain, topk):
    b, s, _ = p3.shape
    tq, kc, sub, rb = DSA_TQ, DSA_KC, Q_BLOCK, DSA_RB
    assert tq == kc and s % tq == 0 and s <= 2 ** 11
    nkc = s // kc
    assert nkc % 2 == 0
    kern = functools.partial(_dsa_kernel, topk=topk)
    return pl.pallas_call(
        kern,
        grid=(b, s // tq),
        in_specs=[
            pl.BlockSpec((None, tq, HR), lambda i, j: (i, j, 0)),
            pl.BlockSpec((None, tq, IQW), lambda i, j: (i, j, OFF_IQ // IQW)),
            pl.BlockSpec((None, s, KV_RANK), lambda i, j: (i, 0, OFF_CKV // KV_RANK)),
            pl.BlockSpec((None, s, LANE), lambda i, j: (i, 0, OFF_IKW // LANE)),
            pl.BlockSpec((None, tq, LANE), lambda i, j: (i, j, OFF_IKW // LANE)),
            pl.BlockSpec((3, ATT_HEADS, sub, sub), lambda i, j: (0, 0, 0, 0)),
            pl.BlockSpec((1, KV_RANK), lambda i, j: (0, 0)),
        ],
        out_specs=pl.BlockSpec((None, tq, HR), lambda i, j: (i, j, 0)),
        out_shape=jax.ShapeDtypeStruct((b, s, HR), BF16),
        scratch_shapes=[
            pltpu.VMEM((s, KV_RANK), BF16),
            pltpu.VMEM((s, IDX_DIM), BF16),
            pltpu.VMEM((ATT_HEADS * tq, KV_RANK), BF16),
            pltpu.VMEM((tq // rb, IDX_HEADS * rb, IDX_DIM), BF16),
            pltpu.VMEM((tq // rb, IDX_HEADS, rb, LANE), F32),
            pltpu.VMEM((nkc, tq, kc), F32),
            pltpu.VMEM((nkc, kc, tq), F32),
            pltpu.VMEM((nkc, ATT_HEADS, tq, kc), F32),
            pltpu.VMEM((ATT_HEADS, tq, LANE), F32),
            pltpu.VMEM((ATT_HEADS, tq, LANE), F32),
            pltpu.VMEM((ATT_HEADS * tq, KV_RANK), F32),
        ],
        compiler_params=_params(2),
        name="dsa",
    )(q3, p3, p3, p3, p3, bias_tiles, kv_gain)


MLSTM_CHUNK = 128


def _mlstm_kernel(mqk_ref, mv_ref, mo_ref, gt_ref, cw_ref, cb_ref, gb_ref, on_ref, o_ref,
                  c_s, n_s, m_s, ext_s):
    lc = MLSTM_CHUNK
    dk, dv = M_QK_DIM, M_V_DIM

    @pl.when(pl.program_id(1) == 0)
    def _():
        c_s[...] = jnp.zeros(c_s.shape, F32)
        n_s[...] = jnp.zeros(n_s.shape, F32)
        m_s[...] = jnp.zeros(m_s.shape, F32)
        ext_s[0:SUBLANE, :] = jnp.zeros((SUBLANE, MQK), F32)

    x = mqk_ref[...]
    ext_s[SUBLANE:SUBLANE + lc, :] = x
    conv = jnp.broadcast_to(cb_ref[...], (lc, MQK))
    for j in range(CONV_WIDTH):
        lo = SUBLANE - (CONV_WIDTH - 1) + j
        conv = conv + cw_ref[j:j + 1, :] * ext_s[lo:lo + lc, :]
    ext_s[0:SUBLANE, :] = x[lc - SUBLANE:lc, :]
    a = conv * _sigmoid(conv)

    gates = gt_ref[...] + gb_ref[...]
    lsg = _log_sigmoid(gates)
    r_i = lax.broadcasted_iota(jnp.int32, (lc, lc), 0)
    c_i = lax.broadcasted_iota(jnp.int32, (lc, lc), 1)
    tril = r_i >= c_i
    hi = lax.Precision.HIGHEST
    bcum_col = jnp.dot(tril.astype(F32), lsg, precision=hi, preferred_element_type=F32)
    gates_t = gates.T
    bcum_row = jnp.dot(lsg.T, (r_i <= c_i).astype(F32), precision=hi, preferred_element_type=F32)

    for h in range(M_HEADS):
        bc = bcum_col[:, M_HEADS + h:M_HEADS + h + 1]
        br = bcum_row[M_HEADS + h:M_HEADS + h + 1, :]
        ic = gates[:, h:h + 1]
        ir = gates_t[h:h + 1, :]
        mst = m_s[h, 0:1, 0:1]
        q = a[:, h * dk:(h + 1) * dk]
        k = a[:, MQK // 2 + h * dk:MQK // 2 + (h + 1) * dk] * (dk ** -0.5)
        v = mv_ref[:, h * dv:(h + 1) * dv]
        qb16, kb16, vb16 = q.astype(BF16), k.astype(BF16), v.astype(BF16)

        dmat = jnp.where(tril, bc - br + ir, -jnp.inf)
        inter = bc + mst
        m_row = jnp.maximum(inter, jnp.max(dmat, axis=-1, keepdims=True))
        w_inter = jnp.exp(inter - m_row)
        qk = lax.dot_general(qb16, kb16, (((1,), (1,)), ((), ())), preferred_element_type=F32)
        qk = qk * jnp.exp(dmat - m_row)
        num = (w_inter * jnp.dot(qb16, c_s[h].astype(BF16), preferred_element_type=F32)
               + jnp.dot(qk.astype(BF16), vb16, preferred_element_type=F32))
        den = (w_inter * jnp.sum(q * n_s[h, 0:1, :], axis=-1, keepdims=True)
               + jnp.sum(qk, axis=-1, keepdims=True))
        hh = num / jnp.maximum(jnp.abs(den), jnp.exp(-m_row))

        b_last = bc[lc - 1:lc, :]
        g_col = b_last - bc + ic
        g_row = b_last - br + ir
        m_new = jnp.maximum(b_last + mst, jnp.max(g_row, axis=-1, keepdims=True))
        decay = jnp.exp(b_last + mst - m_new)
        kw = jnp.exp(g_col - m_new) * k
        c_s[h] = decay * c_s[h] + lax.dot_general(kw.astype(BF16), vb16, (((0,), (0,)), ((), ())),
                                                   preferred_element_type=F32)
        n_s[h, 0:1, :] = decay * n_s[h, 0:1, :] + jnp.sum(kw, axis=0, keepdims=True)
        m_s[h] = jnp.broadcast_to(m_new, (SUBLANE, LANE))

        hn = _rms(hh, on_ref[:, h * dv:(h + 1) * dv])
        o_ref[:, h * dv:(h + 1) * dv] = (hn * _sigmoid(mo_ref[:, h * dv:(h + 1) * dv])).astype(o_ref.dtype)


def _mlstm(p3, conv_w, conv_b, gate_b, out_gain):
    b, s, _ = p3.shape
    lc = MLSTM_CHUNK
    assert s % lc == 0 and lc == LANE
    return pl.pallas_call(
        _mlstm_kernel,
        grid=(b, s // lc),
        in_specs=[
            pl.BlockSpec((None, lc, MQK), lambda i, j: (i, j, OFF_MQK // MQK)),
            pl.BlockSpec((None, lc, MV), lambda i, j: (i, j, OFF_MV // MV)),
            pl.BlockSpec((None, lc, MV), lambda i, j: (i, j, OFF_MO // MV)),
            pl.BlockSpec((None, lc, LANE), lambda i, j: (i, j, OFF_MG // LANE)),
            pl.BlockSpec((CONV_WIDTH, MQK), lambda i, j: (0, 0)),
            pl.BlockSpec((1, MQK), lambda i, j: (0, 0)),
            pl.BlockSpec((1, LANE), lambda i, j: (0, 0)),
            pl.BlockSpec((1, MV), lambda i, j: (0, 0)),
        ],
        out_specs=pl.BlockSpec((None, lc, MV), lambda i, j: (i, j, 0)),
        out_shape=jax.ShapeDtypeStruct((b, s, MV), BF16),
        scratch_shapes=[
            pltpu.VMEM((M_HEADS, M_QK_DIM, M_V_DIM), F32),
            pltpu.VMEM((M_HEADS, SUBLANE, M_QK_DIM), F32),
            pltpu.VMEM((M_HEADS, SUBLANE, LANE), F32),
            pltpu.VMEM((SUBLANE + lc, MQK), F32),
        ],
        compiler_params=_params(2),
        name="mlstm",
    )(p3, p3, p3, p3, conv_w, conv_b, gate_b, out_gain)


def _mergeout_kernel(oa_ref, hm_ref, ga_ref, gm_ref, h_ref, wa_ref, wm_ref, wo_ref, o_ref, mg_s, *, tn):
    d = o_ref.shape[1]
    oa = oa_ref[...]
    hm = hm_ref[...]
    for n in range(d // tn):
        sl = slice(n * tn, (n + 1) * tn)
        ya = jnp.dot(oa, wa_ref[:, sl], preferred_element_type=F32)
        ym = jnp.dot(hm, wm_ref[:, sl], preferred_element_type=F32)
        mg_s[:, sl] = (_sigmoid(ga_ref[:, sl]) * ya + _sigmoid(gm_ref[:, sl]) * ym).astype(BF16)
    mg = mg_s[...]
    for n in range(d // tn):
        sl = slice(n * tn, (n + 1) * tn)
        o_ref[:, sl] = h_ref[:, sl] + jnp.dot(mg, wo_ref[:, sl], preferred_element_type=F32)


def _mergeout(oa, hm, p2, h, wa, wm, wo, l):
    m, d = h.shape
    tm = _tile(m, 256)
    tn = _tile(d, 512)
    assert OFF_GA % d == 0
    ga0 = OFF_GA // d
    resident = functools.partial(pl.BlockSpec, pipeline_mode=pl.Buffered(1))
    return pl.pallas_call(
        functools.partial(_mergeout_kernel, tn=tn),
        grid=(m // tm,),
        in_specs=[
            pl.BlockSpec((tm, HR), lambda i: (i, 0)),
            pl.BlockSpec((tm, MV), lambda i: (i, 0)),
            pl.BlockSpec((tm, d), lambda i: (i, ga0)),
            pl.BlockSpec((tm, d), lambda i: (i, ga0 + 1)),
            pl.BlockSpec((tm, d), lambda i: (i, 0)),
            resident((None, HR, d), lambda i: (l, 0, 0)),
            resident((None, MV, d), lambda i: (l, 0, 0)),
            resident((None, d, d), lambda i: (l, 0, 0)),
        ],
        out_specs=pl.BlockSpec((tm, d), lambda i: (i, 0)),
        out_shape=jax.ShapeDtypeStruct((m, d), F32),
        scratch_shapes=[pltpu.VMEM((tm, d), BF16)],
        compiler_params=_params(1),
        name="mergeout",
    )(oa, hm, p2, p2, h, wa, wm, wo)


def _bias_tiles(rel_bias):
    tq = Q_BLOCK
    i = jnp.arange(tq)[:, None]
    j = jnp.arange(tq)[None, :]

    def bucket(rel):
        n = jnp.maximum(rel, 0)
        max_exact = REL_BUCKETS // 2
        n_large = jnp.maximum(n, max_exact).astype(F32)
        large = max_exact + (jnp.log(n_large / max_exact) / math.log(REL_MAX_DIST / max_exact)
                             * (REL_BUCKETS - max_exact)).astype(jnp.int32)
        large = jnp.minimum(large, REL_BUCKETS - 1)
        return jnp.where(n < max_exact, n, large)

    onehot = jax.nn.one_hot(jnp.stack([bucket(i - j + d * tq) for d in range(3)]), REL_BUCKETS, dtype=F32)
    return jnp.einsum("dijk,kh->dhij", onehot, rel_bias.astype(F32) * LOG2E, precision=lax.Precision.HIGHEST)


def _w_in_kernel(w_ref, o_ref, *, segs):
    for dst, src, width in segs:
        if src is None:
            o_ref[0, :, dst:dst + width] = jnp.zeros((o_ref.shape[1], width), o_ref.dtype)
        else:
            o_ref[0, :, dst:dst + width] = w_ref[0, :, src:src + width].astype(o_ref.dtype)


def _prep_w_in(w_in, d):
    depth, dm, din = w_in.shape
    widths = (HR, KV_RANK, IQW, IDX_DIM, IDX_HEADS, MQK, MV, MV, M_HEADS, M_HEADS, d, d)
    offs = [0]
    for wdt in widths:
        offs.append(offs[-1] + wdt)
    assert offs[-1] == din
    aq, ckv, iq, ik, _, mqk, _, _, mi, _, ga, _ = offs[:12]
    pieces = [(aq, HR), (iq, IQW), (ckv, KV_RANK), (ik, IDX_DIM + IDX_HEADS), (None, LANE - IDX_DIM - IDX_HEADS),
              (mi, 2 * M_HEADS), (None, LANE - 2 * M_HEADS), (mqk, MQK + 2 * MV), (ga, 2 * d)]
    segs = []
    dst = 0
    for src, width in pieces:
        segs.append((dst, src, width))
        dst += width
    tr = _tile(dm, 256)
    return pl.pallas_call(
        functools.partial(_w_in_kernel, segs=tuple(segs)),
        grid=(depth, dm // tr),
        in_specs=[pl.BlockSpec((1, tr, din), lambda l, i: (l, i, 0))],
        out_specs=pl.BlockSpec((1, tr, dst), lambda l, i: (l, i, 0)),
        out_shape=jax.ShapeDtypeStruct((depth, dm, dst), BF16),
        compiler_params=_params(2),
        name="prep_w_in",
    )(w_in)


def kernel(x, rel_bias, ffn1_norm, ffn1_w_gu, ffn1_w_down, mix_norm, w_in, q_norm, kv_norm, conv_w, conv_b,
           igate_b, fgate_b, m_out_norm, w_att_out, w_mem_out, w_out, ffn2_norm, ffn2_w_gu, ffn2_w_down):
    b, s, d = x.shape
    depth = w_in.shape[0]
    topk = min(TOPK_MAX, s // 4)
    m = b * s
    h = x.reshape(m, d)
    bias_tiles = _bias_tiles(rel_bias)
    w1gu, w1d = ffn1_w_gu.astype(BF16), ffn1_w_down.astype(BF16)
    w2gu, w2d = ffn2_w_gu.astype(BF16), ffn2_w_down.astype(BF16)
    w_in_r = _prep_w_in(w_in, d)
    w_att = w_att_out.astype(BF16).reshape(depth, HR, d)
    w_mem = w_mem_out.astype(BF16)
    w_o = w_out.astype(BF16)
    gate_b = jnp.concatenate([igate_b, fgate_b, jnp.zeros((depth, LANE - 2 * M_HEADS), F32)], axis=1)
    for l in range(depth):
        h = _ffn(h, ffn1_norm[l][None], w1gu, w1d, l)

        q2, p2 = _inproj(h, mix_norm[l][None], w_in_r, q_norm[l][None], l)
        p3 = p2.reshape(b, s, p2.shape[1])
        o_att = _dsa(q2.reshape(b, s, HR), p3, bias_tiles, kv_norm[l][None], topk)
        hm = _mlstm(p3, conv_w[l], conv_b[l][None], gate_b[l][None], m_out_norm[l].reshape(1, MV))
        h = _mergeout(o_att.reshape(m, HR), hm.reshape(m, MV), p2, h, w_att, w_mem, w_o, l)

        h = _ffn(h, ffn2_norm[l][None], w2gu, w2d, l)
    return h.reshape(b, s, d)
```

```python
import functools
import math

import jax
import jax.numpy as jnp
from jax import lax
from jax.experimental import pallas as pl
from jax.experimental.pallas import tpu as pltpu

F32 = jnp.float32
BF16 = jnp.bfloat16

ATT_HEADS = 8
KV_RANK = 256
IDX_HEADS = 8
IDX_DIM = 64
TOPK_MAX = 256
Q_BLOCK = 128
REL_BUCKETS = 32
REL_MAX_DIST = 128
M_HEADS = 4
M_QK_DIM = 128
M_V_DIM = 256
CONV_WIDTH = 4
EPS = 1e-6

HR = ATT_HEADS * KV_RANK
IQW = IDX_HEADS * IDX_DIM
MQK = 2 * M_HEADS * M_QK_DIM
MV = M_HEADS * M_V_DIM
LANE = 128
SUBLANE = 8
VMEM_LIMIT = 56 * 1024 * 1024

OFF_IQ = 0
OFF_CKV = OFF_IQ + IQW
OFF_IKW = OFF_CKV + KV_RANK
OFF_MG = OFF_IKW + LANE
OFF_MQK = OFF_MG + LANE
OFF_MV = OFF_MQK + MQK
OFF_MO = OFF_MV + MV
OFF_GA = OFF_MO + MV

KEY_NEG_INF = -2139095041
LOG2E = math.log2(math.e)
Q_SCALE = KV_RANK ** -0.5 * LOG2E
F32_LOWEST = float(jnp.finfo(jnp.float32).min)


def _params(n_grid):
    return pltpu.CompilerParams(dimension_semantics=("arbitrary",) * n_grid,
                                vmem_limit_bytes=VMEM_LIMIT)


def _tile(n, pref):
    if n <= pref:
        return n
    t = (pref // LANE) * LANE
    while n % t:
        t -= LANE
    return t


def _rms(x, g):
    return x * lax.rsqrt(jnp.mean(x * x, axis=-1, keepdims=True) + EPS) * g


def _sigmoid(x):
    return 1.0 / (1.0 + jnp.exp(-x))


def _log_sigmoid(x):
    return jnp.minimum(x, 0.0) - jnp.log(1.0 + jnp.exp(-jnp.abs(x)))


def _ffn_kernel(h_ref, g_ref, wg_ref, wu_ref, wd_ref, o_ref, xn_ref, *, overlap):
    j = pl.program_id(1)

    @pl.when(j == 0)
    def _():
        x = h_ref[...]
        xn_ref[...] = _rms(x, g_ref[...]).astype(BF16)
        o_ref[...] = x

    xn = xn_ref[...]
    g = jnp.dot(xn, wg_ref[0], preferred_element_type=F32)
    u = jnp.dot(xn, wu_ref[0], preferred_element_type=F32)
    a = g * _sigmoid(g) * u * 0.5
    if overlap:
        col = lax.broadcasted_iota(jnp.int32, a.shape, 1)
        a = jnp.where((j == pl.num_programs(1) - 1) & (col < overlap), 0.0, a)
    o_ref[...] += jnp.dot(a.astype(BF16), wd_ref[0], preferred_element_type=F32)


def _ffn(h, gain, w_gu, w_down, l):
    m, d = h.shape
    f = w_down.shape[1]
    tm = _tile(m, 1024)
    tf = min(f, 512)
    nj = -(-f // tf)
    assert f % LANE == 0 and tf % LANE == 0

    def off(j):
        return pl.multiple_of(jnp.minimum(j * tf, f - tf), LANE)

    return pl.pallas_call(
        functools.partial(_ffn_kernel, overlap=nj * tf - f),
        grid=(m // tm, nj),
        in_specs=[
            pl.BlockSpec((tm, d), lambda i, j: (i, 0)),
            pl.BlockSpec((1, d), lambda i, j: (0, 0)),
            pl.BlockSpec((pl.Element(1), pl.Element(d), pl.Element(tf)), lambda i, j: (l, 0, off(j))),
            pl.BlockSpec((pl.Element(1), pl.Element(d), pl.Element(tf)),
                         lambda i, j: (l, 0, pl.multiple_of(f + off(j), LANE))),
            pl.BlockSpec((pl.Element(1), pl.Element(tf), pl.Element(d)), lambda i, j: (l, off(j), 0)),
        ],
        out_specs=pl.BlockSpec((tm, d), lambda i, j: (i, 0)),
        out_shape=jax.ShapeDtypeStruct((m, d), F32),
        scratch_shapes=[pltpu.VMEM((tm, d), BF16)],
        compiler_params=_params(2),
        name="ffn",
    )(h, gain, w_gu, w_gu, w_down)


def _inproj_kernel(h_ref, g_ref, w_ref, qn_ref, q_ref, p_ref, xn_ref, *, nq):
    j = pl.program_id(1)
    nt_dims = (((1,), (1,)), ((), ()))

    @pl.when(j == 0)
    def _():
        xn_ref[...] = _rms(h_ref[...], g_ref[...]).astype(BF16)

    @pl.when(j < nq)
    def _():
        acc = lax.dot_general(xn_ref[...], w_ref[...], nt_dims, preferred_element_type=F32)
        for k in range(acc.shape[1] // KV_RANK):
            sl = slice(k * KV_RANK, (k + 1) * KV_RANK)
            q_ref[:, sl] = (_rms(acc[:, sl], qn_ref[...]) * Q_SCALE).astype(q_ref.dtype)

    @pl.when(j >= nq)
    def _():
        p_ref[...] = lax.dot_general(xn_ref[...], w_ref[...], nt_dims, preferred_element_type=F32)


def _inproj(h, gain, w, q_gain, l):
    m, d = h.shape
    n = w.shape[1]
    tm = _tile(m, 1024)
    tn = 1024 if n % 1024 == 0 else 512
    assert HR % tn == 0 and n % tn == 0 and tn % KV_RANK == 0
    nq = HR // tn
    return pl.pallas_call(
        functools.partial(_inproj_kernel, nq=nq),
        grid=(m // tm, n // tn),
        in_specs=[
            pl.BlockSpec((tm, d), lambda i, j: (i, 0)),
            pl.BlockSpec((1, d), lambda i, j: (0, 0)),
            pl.BlockSpec((None, tn, d), lambda i, j: (l, j, 0)),
            pl.BlockSpec((1, KV_RANK), lambda i, j: (0, 0)),
        ],
        out_specs=[
            pl.BlockSpec((tm, tn), lambda i, j: (i, jnp.minimum(j, nq - 1))),
            pl.BlockSpec((tm, tn), lambda i, j: (i, jnp.maximum(j - nq, 0))),
        ],
        out_shape=[jax.ShapeDtypeStruct((m, HR), BF16), jax.ShapeDtypeStruct((m, n - HR), F32)],
        scratch_shapes=[pltpu.VMEM((tm, d), BF16)],
        compiler_params=_params(2),
        name="inproj",
    )(h, gain, w, q_gain)


DSA_TQ = 256
DSA_KC = 256
DSA_RB = 64


def _dsa_kernel(q_ref, iq_ref, ckv_ref, ik_ref, iw_ref, bias_ref, kn_ref, o_ref,
                ckv_s, ik_s, q_s, iq_s, wb_s, sc_s, sct_s, lg_s, m_s, l_s, acc_s, *, topk):
    qi = pl.program_id(1)
    tq, kc, nh, sub, rb = DSA_TQ, DSA_KC, ATT_HEADS, Q_BLOCK, DSA_RB
    nrb = tq // rb
    nchunk = qi + 1

    @pl.when(qi == 0)
    def _():
        ckv_s[...] = _rms(ckv_ref[...], kn_ref[...]).astype(BF16)
        ik_s[...] = ik_ref[:, :IDX_DIM].astype(BF16)

    for h in range(nh):
        q_s[h * tq:(h + 1) * tq, :] = q_ref[:, h * KV_RANK:(h + 1) * KV_RANK]
    w = iw_ref[:, IDX_DIM:IDX_DIM + IDX_HEADS] * (IQW ** -0.5)
    for r in range(nrb):
        for h in range(IDX_HEADS):
            iq_s[r, h * rb:(h + 1) * rb, :] = iq_ref[r * rb:(r + 1) * rb, h * IDX_DIM:(h + 1) * IDX_DIM].astype(BF16)
            wb_s[r, h] = jnp.broadcast_to(w[r * rb:(r + 1) * rb, h:h + 1], (rb, LANE))

    nt_dims = (((1,), (1,)), ((), ()))
    row_rb = lax.broadcasted_iota(jnp.int32, (rb, kc), 0)
    col_rb = lax.broadcasted_iota(jnp.int32, (rb, kc), 1)

    last = nchunk - 1

    def for_chunk_pairs(body, start):
        def pair(p, carry):
            c0 = start + 2 * p
            body(c0)
            body(jnp.minimum(c0 + 1, last))
            return carry

        lax.fori_loop(0, (nchunk - start + 1) // 2, pair, 0)

    def score_chunk(c):
        off = pl.multiple_of(c * kc, kc)
        ikc = ik_s[pl.ds(off, kc), :]
        for r in range(nrb):
            d = lax.dot_general(iq_s[r], ikc, nt_dims, preferred_element_type=F32)
            d = jnp.maximum(d, 0.0).reshape(IDX_HEADS, rb, kc)
            wb = wb_s[r]
            s = jnp.sum(d * jnp.concatenate([wb, wb], axis=-1), axis=0)
            causal = (c * kc + col_rb) <= (qi * tq + r * rb + row_rb)
            sc_s[c, r * rb:(r + 1) * rb, :] = jnp.where(causal, s, -jnp.inf)
        sct_s[c] = sc_s[c].T

    for_chunk_pairs(score_chunk, 0)

    npair = (nchunk + 1) // 2

    @pl.when(nchunk % 2 == 1)
    def _():
        sct_s[nchunk] = jnp.full((kc, tq), -jnp.inf, F32)

    kf = float(topk)

    def count(pred):
        def pair_body(p, acc):
            for c in (2 * p, 2 * p + 1):
                hit = jnp.where(pred(sct_s[c].reshape(kc // 32, 4, SUBLANE, tq)), 1.0, 0.0)
                acc = acc + jnp.sum(jnp.sum(hit, axis=1), axis=0)
            return acc

        acc = lax.fori_loop(0, npair, pair_body, jnp.zeros((SUBLANE, tq), F32))
        return jnp.sum(acc, axis=0, keepdims=True)

    def key_to_float(key):
        return lax.bitcast_convert_type(jnp.where(key >= 0, key, key ^ jnp.int32(0x7FFFFFFF)), F32)

    def bit_body(it, t_key):
        cand = t_key + lax.shift_left(jnp.int32(1), 31 - it)
        cf = key_to_float(cand)
        cnt = count(lambda st: st >= cf[None, None])
        ok = (cnt >= kf) | (cand < KEY_NEG_INF)
        return jnp.where(ok, cand, t_key)

    t_key = lax.fori_loop(0, 32, bit_body, jnp.full((SUBLANE, tq), -2 ** 31, jnp.int32))
    thr_row = jnp.maximum(key_to_float(t_key), F32_LOWEST)

    def to_rows(v):
        return jnp.broadcast_to(v[0:1, :], (kc, tq)).T

    n_gt = count(lambda st: st > thr_row[None, None])
    n_ge = count(lambda st: st >= thr_row[None, None])
    need = kf - n_gt

    @pl.when(jnp.max(n_ge - kf) > 0.0)
    def _():
        kidx_t = lax.broadcasted_iota(jnp.int32, (kc, tq), 0).astype(F32)

        def mark_body(c, carry):
            sct_s[c] = jnp.where(sct_s[c] == thr_row[0:1, :], kidx_t + (c * kc).astype(F32), jnp.inf)
            return carry

        lax.fori_loop(0, 2 * npair, mark_body, 0)

        def ybit_body(it, y):
            cand = y + lax.shift_left(jnp.int32(1), 11 - it)
            cnt = count(lambda tk: tk < cand.astype(F32)[None, None])
            return jnp.where(cnt < need, cand, y)

        y = lax.fori_loop(0, 12, ybit_body, jnp.zeros((SUBLANE, tq), jnp.int32))
        y_rows = to_rows(y.astype(F32))
        thr_rows = to_rows(thr_row)
        kcol = lax.broadcasted_iota(jnp.int32, (tq, kc), 1).astype(F32)

        def drop_body(c, carry):
            s = sc_s[c]
            drop = (s == thr_rows) & ((kcol + (c * kc).astype(F32)) > y_rows)
            sc_s[c] = jnp.where(drop, -jnp.inf, s)
            return carry

        lax.fori_loop(0, nchunk, drop_body, 0)

    thr = to_rows(thr_row)


    def bias_tile(c, a, b):
        return bias_ref[jnp.clip(2 * qi + a - 2 * c - b, 0, 2)]

    def logit_chunk(c, first=False):
        off = pl.multiple_of(c * kc, kc)
        lg = lax.dot_general(q_s[...], ckv_s[pl.ds(off, kc), :], nt_dims, preferred_element_type=F32)
        bias = jnp.concatenate(
            [jnp.concatenate([bias_tile(c, a, 0), bias_tile(c, a, 1)], axis=-1) for a in range(tq // sub)],
            axis=-2)
        keep = sc_s[c] >= thr
        lg = jnp.where(keep[None], lg.reshape(nh, tq, kc) + bias, -jnp.inf)
        lg_s[c] = lg
        mx = jnp.maximum(lg[..., :LANE], lg[..., LANE:])
        m_s[...] = mx if first else jnp.maximum(m_s[...], mx)

    logit_chunk(0, first=True)
    for_chunk_pairs(logit_chunk, 1)

    m = jnp.max(m_s[...], axis=-1, keepdims=True)
    m_s[...] = jnp.broadcast_to(m, m_s.shape)

    def pv_chunk(c, first=False):
        off = pl.multiple_of(c * kc, kc)
        lg = lg_s[c]
        mb = m_s[...]
        p0 = jnp.exp2(lg[..., :LANE] - mb)
        p1 = jnp.exp2(lg[..., LANE:] - mb)
        p = jnp.concatenate([p0, p1], axis=-1).reshape(nh * tq, kc).astype(BF16)
        pv = jnp.dot(p, ckv_s[pl.ds(off, kc), :], preferred_element_type=F32)
        if first:
            l_s[...] = p0 + p1
            acc_s[...] = pv
        else:
            l_s[...] += p0 + p1
            acc_s[...] += pv

    pv_chunk(0, first=True)

    def pv_pair(p, carry):
        pv_chunk(1 + 2 * p)
        pv_chunk(2 + 2 * p)
        return carry

    lax.fori_loop(0, (nchunk - 1) // 2, pv_pair, 0)

    @pl.when(nchunk % 2 == 0)
    def _():
        pv_chunk(last)

    lsum = jnp.sum(l_s[...], axis=-1, keepdims=True)
    for h in range(nh):
        o_ref[:, h * KV_RANK:(h + 1) * KV_RANK] = (acc_s[h * tq:(h + 1) * tq, :] / lsum[h]).astype(o_ref.dtype)


def _dsa(q3, p3, bias_tiles, kv_gain, topk):
    b, s, _ = p3.shape
    tq, kc, sub, rb = DSA_TQ, DSA_KC, Q_BLOCK, DSA_RB
    assert tq == kc and s % tq == 0 and s <= 2 ** 11
    nkc = s // kc
    assert nkc % 2 == 0
    kern = functools.partial(_dsa_kernel, topk=topk)
    return pl.pallas_call(
        kern,
        grid=(b, s // tq),
        in_specs=[
            pl.BlockSpec((None, tq, HR), lambda i, j: (i, j, 0)),
            pl.BlockSpec((None, tq, IQW), lambda i, j: (i, j, OFF_IQ // IQW)),
            pl.BlockSpec((None, s, KV_RANK), lambda i, j: (i, 0, OFF_CKV // KV_RANK)),
            pl.BlockSpec((None, s, LANE), lambda i, j: (i, 0, OFF_IKW // LANE)),
            pl.BlockSpec((None, tq, LANE), lambda i, j: (i, j, OFF_IKW // LANE)),
            pl.BlockSpec((3, ATT_HEADS, sub, sub), lambda i, j: (0, 0, 0, 0)),
            pl.BlockSpec((1, KV_RANK), lambda i, j: (0, 0)),
        ],
        out_specs=pl.BlockSpec((None, tq, HR), lambda i, j: (i, j, 0)),
        out_shape=jax.ShapeDtypeStruct((b, s, HR), BF16),
        scratch_shapes=[
            pltpu.VMEM((s, KV_RANK), BF16),
            pltpu.VMEM((s, IDX_DIM), BF16),
            pltpu.VMEM((ATT_HEADS * tq, KV_RANK), BF16),
            pltpu.VMEM((tq // rb, IDX_HEADS * rb, IDX_DIM), BF16),
            pltpu.VMEM((tq // rb, IDX_HEADS, rb, LANE), F32),
            pltpu.VMEM((nkc, tq, kc), F32),
            pltpu.VMEM((nkc, kc, tq), F32),
            pltpu.VMEM((nkc, ATT_HEADS, tq, kc), F32),
            pltpu.VMEM((ATT_HEADS, tq, LANE), F32),
            pltpu.VMEM((ATT_HEADS, tq, LANE), F32),
            pltpu.VMEM((ATT_HEADS * tq, KV_RANK), F32),
        ],
        compiler_params=_params(2),
        name="dsa",
    )(q3, p3, p3, p3, p3, bias_tiles, kv_gain)


MLSTM_CHUNK = 128


def _mlstm_kernel(mqk_ref, mv_ref, mo_ref, gt_ref, cw_ref, cb_ref, gb_ref, on_ref, o_ref,
                  c_s, n_s, m_s, ext_s):
    lc = MLSTM_CHUNK
    dk, dv = M_QK_DIM, M_V_DIM

    @pl.when(pl.program_id(1) == 0)
    def _():
        c_s[...] = jnp.zeros(c_s.shape, F32)
        n_s[...] = jnp.zeros(n_s.shape, F32)
        m_s[...] = jnp.zeros(m_s.shape, F32)
        ext_s[0:SUBLANE, :] = jnp.zeros((SUBLANE, MQK), F32)

    x = mqk_ref[...]
    ext_s[SUBLANE:SUBLANE + lc, :] = x
    conv = jnp.broadcast_to(cb_ref[...], (lc, MQK))
    for j in range(CONV_WIDTH):
        lo = SUBLANE - (CONV_WIDTH - 1) + j
        conv = conv + cw_ref[j:j + 1, :] * ext_s[lo:lo + lc, :]
    ext_s[0:SUBLANE, :] = x[lc - SUBLANE:lc, :]
    a = conv * _sigmoid(conv)

    gates = gt_ref[...] + gb_ref[...]
    lsg = _log_sigmoid(gates)
    r_i = lax.broadcasted_iota(jnp.int32, (lc, lc), 0)
    c_i = lax.broadcasted_iota(jnp.int32, (lc, lc), 1)
    tril = r_i >= c_i
    hi = lax.Precision.HIGHEST
    bcum_col = jnp.dot(tril.astype(F32), lsg, precision=hi, preferred_element_type=F32)
    gates_t = gates.T
    bcum_row = jnp.dot(lsg.T, (r_i <= c_i).astype(F32), precision=hi, preferred_element_type=F32)

    for h in range(M_HEADS):
        bc = bcum_col[:, M_HEADS + h:M_HEADS + h + 1]
        br = bcum_row[M_HEADS + h:M_HEADS + h + 1, :]
        ic = gates[:, h:h + 1]
        ir = gates_t[h:h + 1, :]
        mst = m_s[h, 0:1, 0:1]
        q = a[:, h * dk:(h + 1) * dk]
        k = a[:, MQK // 2 + h * dk:MQK // 2 + (h + 1) * dk] * (dk ** -0.5)
        v = mv_ref[:, h * dv:(h + 1) * dv]
        qb16, kb16, vb16 = q.astype(BF16), k.astype(BF16), v.astype(BF16)

        dmat = jnp.where(tril, bc - br + ir, -jnp.inf)
        inter = bc + mst
        m_row = jnp.maximum(inter, jnp.max(dmat, axis=-1, keepdims=True))
        w_inter = jnp.exp(inter - m_row)
        qk = lax.dot_general(qb16, kb16, (((1,), (1,)), ((), ())), preferred_element_type=F32)
        qk = qk * jnp.exp(dmat - m_row)
        num = (w_inter * jnp.dot(qb16, c_s[h].astype(BF16), preferred_element_type=F32)
               + jnp.dot(qk.astype(BF16), vb16, preferred_element_type=F32))
        den = (w_inter * jnp.sum(q * n_s[h, 0:1, :], axis=-1, keepdims=True)
               + jnp.sum(qk, axis=-1, keepdims=True))
        hh = num / jnp.maximum(jnp.abs(den), jnp.exp(-m_row))

        b_last = bc[lc - 1:lc, :]
        g_col = b_last - bc + ic
        g_row = b_last - br + ir
        m_new = jnp.maximum(b_last + mst, jnp.max(g_row, axis=-1, keepdims=True))
        decay = jnp.exp(b_last + mst - m_new)
        kw = jnp.exp(g_col - m_new) * k
        c_s[h] = decay * c_s[h] + lax.dot_general(kw.astype(BF16), vb16, (((0,), (0,)), ((), ())),
                                                   preferred_element_type=F32)
        n_s[h, 0:1, :] = decay * n_s[h, 0:1, :] + jnp.sum(kw, axis=0, keepdims=True)
        m_s[h] = jnp.broadcast_to(m_new, (SUBLANE, LANE))

        hn = _rms(hh, on_ref[:, h * dv:(h + 1) * dv])
        o_ref[:, h * dv:(h + 1) * dv] = (hn * _sigmoid(mo_ref[:, h * dv:(h + 1) * dv])).astype(o_ref.dtype)


def _mlstm(p3, conv_w, conv_b, gate_b, out_gain):
    b, s, _ = p3.shape
    lc = MLSTM_CHUNK
    assert s % lc == 0 and lc == LANE
    return pl.pallas_call(
        _mlstm_kernel,
        grid=(b, s // lc),
        in_specs=[
            pl.BlockSpec((None, lc, MQK), lambda i, j: (i, j, OFF_MQK // MQK)),
            pl.BlockSpec((None, lc, MV), lambda i, j: (i, j, OFF_MV // MV)),
            pl.BlockSpec((None, lc, MV), lambda i, j: (i, j, OFF_MO // MV)),
            pl.BlockSpec((None, lc, LANE), lambda i, j: (i, j, OFF_MG // LANE)),
            pl.BlockSpec((CONV_WIDTH, MQK), lambda i, j: (0, 0)),
            pl.BlockSpec((1, MQK), lambda i, j: (0, 0)),
            pl.BlockSpec((1, LANE), lambda i, j: (0, 0)),
            pl.BlockSpec((1, MV), lambda i, j: (0, 0)),
        ],
        out_specs=pl.BlockSpec((None, lc, MV), lambda i, j: (i, j, 0)),
        out_shape=jax.ShapeDtypeStruct((b, s, MV), BF16),
        scratch_shapes=[
            pltpu.VMEM((M_HEADS, M_QK_DIM, M_V_DIM), F32),
            pltpu.VMEM((M_HEADS, SUBLANE, M_QK_DIM), F32),
            pltpu.VMEM((M_HEADS, SUBLANE, LANE), F32),
            pltpu.VMEM((SUBLANE + lc, MQK), F32),
        ],
        compiler_params=_params(2),
        name="mlstm",
    )(p3, p3, p3, p3, conv_w, conv_b, gate_b, out_gain)


def _mergeout_kernel(oa_ref, hm_ref, ga_ref, gm_ref, h_ref, wa_ref, wm_ref, wo_ref, o_ref, mg_s, *, tn):
    d = o_ref.shape[1]
    oa = oa_ref[...]
    hm = hm_ref[...]
    for n in range(d // tn):
        sl = slice(n * tn, (n + 1) * tn)
        ya = jnp.dot(oa, wa_ref[:, sl], preferred_element_type=F32)
        ym = jnp.dot(hm, wm_ref[:, sl], preferred_element_type=F32)
        mg_s[:, sl] = (_sigmoid(ga_ref[:, sl]) * ya + _sigmoid(gm_ref[:, sl]) * ym).astype(BF16)
    mg = mg_s[...]
    for n in range(d // tn):
        sl = slice(n * tn, (n + 1) * tn)
        o_ref[:, sl] = h_ref[:, sl] + jnp.dot(mg, wo_ref[:, sl], preferred_element_type=F32)


def _mergeout(oa, hm, p2, h, wa, wm, wo, l):
    m, d = h.shape
    tm = _tile(m, 256)
    tn = _tile(d, 512)
    assert OFF_GA % d == 0
    ga0 = OFF_GA // d
    resident = functools.partial(pl.BlockSpec, pipeline_mode=pl.Buffered(1))
    return pl.pallas_call(
        functools.partial(_mergeout_kernel, tn=tn),
        grid=(m // tm,),
        in_specs=[
            pl.BlockSpec((tm, HR), lambda i: (i, 0)),
            pl.BlockSpec((tm, MV), lambda i: (i, 0)),
            pl.BlockSpec((tm, d), lambda i: (i, ga0)),
            pl.BlockSpec((tm, d), lambda i: (i, ga0 + 1)),
            pl.BlockSpec((tm, d), lambda i: (i, 0)),
            resident((None, HR, d), lambda i: (l, 0, 0)),
            resident((None, MV, d), lambda i: (l, 0, 0)),
            resident((None, d, d), lambda i: (l, 0, 0)),
        ],
        out_specs=pl.BlockSpec((tm, d), lambda i: (i, 0)),
        out_shape=jax.ShapeDtypeStruct((m, d), F32),
        scratch_shapes=[pltpu.VMEM((tm, d), BF16)],
        compiler_params=_params(1),
        name="mergeout",
    )(oa, hm, p2, p2, h, wa, wm, wo)


def _bias_tiles(rel_bias):
    tq = Q_BLOCK
    i = jnp.arange(tq)[:, None]
    j = jnp.arange(tq)[None, :]

    def bucket(rel):
        n = jnp.maximum(rel, 0)
        max_exact = REL_BUCKETS // 2
        n_large = jnp.maximum(n, max_exact).astype(F32)
        large = max_exact + (jnp.log(n_large / max_exact) / math.log(REL_MAX_DIST / max_exact)
                             * (REL_BUCKETS - max_exact)).astype(jnp.int32)
        large = jnp.minimum(large, REL_BUCKETS - 1)
        return jnp.where(n < max_exact, n, large)

    onehot = jax.nn.one_hot(jnp.stack([bucket(i - j + d * tq) for d in range(3)]), REL_BUCKETS, dtype=F32)
    return jnp.einsum("dijk,kh->dhij", onehot, rel_bias.astype(F32) * LOG2E, precision=lax.Precision.HIGHEST)


def _w_in_kernel(w_ref, o_ref, *, segs):
    for dst, src, width in segs:
        if src is None:
            o_ref[0, dst:dst + width, :] = jnp.zeros((width, o_ref.shape[2]), o_ref.dtype)
        else:
            o_ref[0, dst:dst + width, :] = w_ref[0, src:src + width, :].astype(o_ref.dtype)


def _prep_w_in(w_in, d):
    depth, dm, din = w_in.shape
    widths = (HR, KV_RANK, IQW, IDX_DIM, IDX_HEADS, MQK, MV, MV, M_HEADS, M_HEADS, d, d)
    offs = [0]
    for wdt in widths:
        offs.append(offs[-1] + wdt)
    assert offs[-1] == din
    aq, ckv, iq, ik, _, mqk, _, _, mi, _, ga, _ = offs[:12]
    pieces = [(aq, HR), (iq, IQW), (ckv, KV_RANK), (ik, IDX_DIM + IDX_HEADS), (None, LANE - IDX_DIM - IDX_HEADS),
              (mi, 2 * M_HEADS), (None, LANE - 2 * M_HEADS), (mqk, MQK + 2 * MV), (ga, 2 * d)]
    segs = []
    dst = 0
    for src, width in pieces:
        segs.append((dst, src, width))
        dst += width
    assert all(o % SUBLANE == 0 and (src is None or src % SUBLANE == 0) for o, src, _ in segs)
    tc = _tile(dm, 256)
    return pl.pallas_call(
        functools.partial(_w_in_kernel, segs=tuple(segs)),
        grid=(depth, dm // tc),
        in_specs=[pl.BlockSpec((1, din, tc), lambda l, i: (l, 0, i))],
        out_specs=pl.BlockSpec((1, dst, tc), lambda l, i: (l, 0, i)),
        out_shape=jax.ShapeDtypeStruct((depth, dst, dm), BF16),
        compiler_params=_params(2),
        name="prep_w_in",
    )(jnp.swapaxes(w_in, 1, 2))


def kernel(x, rel_bias, ffn1_norm, ffn1_w_gu, ffn1_w_down, mix_norm, w_in, q_norm, kv_norm, conv_w, conv_b,
           igate_b, fgate_b, m_out_norm, w_att_out, w_mem_out, w_out, ffn2_norm, ffn2_w_gu, ffn2_w_down):
    b, s, d = x.shape
    depth = w_in.shape[0]
    topk = min(TOPK_MAX, s // 4)
    m = b * s
    h = x.reshape(m, d)
    bias_tiles = _bias_tiles(rel_bias)
    w1gu, w1d = ffn1_w_gu.astype(BF16), ffn1_w_down.astype(BF16)
    w2gu, w2d = ffn2_w_gu.astype(BF16), ffn2_w_down.astype(BF16)
    w_in_r = _prep_w_in(w_in, d)
    w_att = w_att_out.astype(BF16).reshape(depth, HR, d)
    w_mem = w_mem_out.astype(BF16)
    w_o = w_out.astype(BF16)
    gate_b = jnp.concatenate([igate_b, fgate_b, jnp.zeros((depth, LANE - 2 * M_HEADS), F32)], axis=1)
    for l in range(depth):
        h = _ffn(h, ffn1_norm[l][None], w1gu, w1d, l)

        q2, p2 = _inproj(h, mix_norm[l][None], w_in_r, q_norm[l][None], l)
        p3 = p2.reshape(b, s, p2.shape[1])
        o_att = _dsa(q2.reshape(b, s, HR), p3, bias_tiles, kv_norm[l][None], topk)
        hm = _mlstm(p3, conv_w[l], conv_b[l][None], gate_b[l][None], m_out_norm[l].reshape(1, MV))
        h = _mergeout(o_att.reshape(m, HR), hm.reshape(m, MV), p2, h, w_att, w_mem, w_o, l)

        h = _ffn(h, ffn2_norm[l][None], w2gu, w2d, l)
    return h.reshape(b, s, d)
```

```python
import functools
import math

import jax
import jax.numpy as jnp
from jax import lax
from jax.experimental import pallas as pl
from jax.experimental.pallas import tpu as pltpu

F32 = jnp.float32
BF16 = jnp.bfloat16

ATT_HEADS = 8
KV_RANK = 256
IDX_HEADS = 8
IDX_DIM = 64
TOPK_MAX = 256
Q_BLOCK = 128
REL_BUCKETS = 32
REL_MAX_DIST = 128
M_HEADS = 4
M_QK_DIM = 128
M_V_DIM = 256
CONV_WIDTH = 4
EPS = 1e-6

HR = ATT_HEADS * KV_RANK
IQW = IDX_HEADS * IDX_DIM
MQK = 2 * M_HEADS * M_QK_DIM
MV = M_HEADS * M_V_DIM
LANE = 128
SUBLANE = 8
VMEM_LIMIT = 56 * 1024 * 1024
FFN_VMEM_LIMIT = 60 * 1024 * 1024

OFF_IQ = 0
OFF_CKV = OFF_IQ + IQW
OFF_IKW = OFF_CKV + KV_RANK
OFF_MG = OFF_IKW + LANE
OFF_MQK = OFF_MG + LANE
OFF_MV = OFF_MQK + MQK
OFF_MO = OFF_MV + MV
OFF_GA = OFF_MO + MV

KEY_NEG_INF = -2139095041
LOG2E = math.log2(math.e)
Q_SCALE = KV_RANK ** -0.5 * LOG2E
F32_LOWEST = float(jnp.finfo(jnp.float32).min)


def _params(n_grid, vmem_limit=VMEM_LIMIT):
    return pltpu.CompilerParams(dimension_semantics=("arbitrary",) * n_grid,
                                vmem_limit_bytes=vmem_limit)


def _tile(n, pref):
    if n <= pref:
        return n
    t = (pref // LANE) * LANE
    while n % t:
        t -= LANE
    return t


def _rms(x, g):
    return x * lax.rsqrt(jnp.mean(x * x, axis=-1, keepdims=True) + EPS) * g


def _sigmoid(x):
    return 1.0 / (1.0 + jnp.exp(-x))


def _log_sigmoid(x):
    return jnp.minimum(x, 0.0) - jnp.log(1.0 + jnp.exp(-jnp.abs(x)))


def _ffn_kernel(h_ref, g_ref, wg_ref, wu_ref, wd_ref, o_ref, xn_ref, *, overlap):
    j = pl.program_id(1)

    @pl.when(j == 0)
    def _():
        x = h_ref[...]
        xn_ref[...] = _rms(x, g_ref[...]).astype(BF16)
        o_ref[...] = x

    xn = xn_ref[...]
    g = jnp.dot(xn, wg_ref[0], preferred_element_type=F32)
    u = jnp.dot(xn, wu_ref[0], preferred_element_type=F32)
    a = g * _sigmoid(g) * u * 0.5
    if overlap:
        col = lax.broadcasted_iota(jnp.int32, a.shape, 1)
        a = jnp.where((j == pl.num_programs(1) - 1) & (col < overlap), 0.0, a)
    o_ref[...] += jnp.dot(a.astype(BF16), wd_ref[0].astype(BF16), preferred_element_type=F32)


def _ffn(h, gain, w_gu, w_down, l):
    m, d = h.shape
    f = w_down.shape[1]
    tm = _tile(m, 1024)
    tf = min(f, 512)
    nj = -(-f // tf)
    assert f % LANE == 0 and tf % LANE == 0

    def off(j):
        return pl.multiple_of(jnp.minimum(j * tf, f - tf), LANE)

    return pl.pallas_call(
        functools.partial(_ffn_kernel, overlap=nj * tf - f),
        grid=(m // tm, nj),
        in_specs=[
            pl.BlockSpec((tm, d), lambda i, j: (i, 0)),
            pl.BlockSpec((1, d), lambda i, j: (0, 0)),
            pl.BlockSpec((pl.Element(1), pl.Element(d), pl.Element(tf)), lambda i, j: (l, 0, off(j))),
            pl.BlockSpec((pl.Element(1), pl.Element(d), pl.Element(tf)),
                         lambda i, j: (l, 0, pl.multiple_of(f + off(j), LANE))),
            pl.BlockSpec((pl.Element(1), pl.Element(tf), pl.Element(d)), lambda i, j: (l, off(j), 0)),
        ],
        out_specs=pl.BlockSpec((tm, d), lambda i, j: (i, 0)),
        out_shape=jax.ShapeDtypeStruct((m, d), F32),
        scratch_shapes=[pltpu.VMEM((tm, d), BF16)],
        compiler_params=_params(2, FFN_VMEM_LIMIT),
        name="ffn",
    )(h, gain, w_gu, w_gu, w_down)


def _inproj_kernel(h_ref, g_ref, w_ref, qn_ref, q_ref, p_ref, xn_ref, *, nq):
    j = pl.program_id(1)
    nt_dims = (((1,), (1,)), ((), ()))

    @pl.when(j == 0)
    def _():
        xn_ref[...] = _rms(h_ref[...], g_ref[...]).astype(BF16)

    @pl.when(j < nq)
    def _():
        acc = lax.dot_general(xn_ref[...], w_ref[...], nt_dims, preferred_element_type=F32)
        for k in range(acc.shape[1] // KV_RANK):
            sl = slice(k * KV_RANK, (k + 1) * KV_RANK)
            q_ref[:, sl] = (_rms(acc[:, sl], qn_ref[...]) * Q_SCALE).astype(q_ref.dtype)

    @pl.when(j >= nq)
    def _():
        p_ref[...] = lax.dot_general(xn_ref[...], w_ref[...], nt_dims, preferred_element_type=F32)


def _inproj(h, gain, w, q_gain, l):
    m, d = h.shape
    n = w.shape[1]
    tm = _tile(m, 1024)
    tn = 1024 if n % 1024 == 0 else 512
    assert HR % tn == 0 and n % tn == 0 and tn % KV_RANK == 0
    nq = HR // tn
    return pl.pallas_call(
        functools.partial(_inproj_kernel, nq=nq),
        grid=(m // tm, n // tn),
        in_specs=[
            pl.BlockSpec((tm, d), lambda i, j: (i, 0)),
            pl.BlockSpec((1, d), lambda i, j: (0, 0)),
            pl.BlockSpec((None, tn, d), lambda i, j: (l, j, 0)),
            pl.BlockSpec((1, KV_RANK), lambda i, j: (0, 0)),
        ],
        out_specs=[
            pl.BlockSpec((tm, tn), lambda i, j: (i, jnp.minimum(j, nq - 1))),
            pl.BlockSpec((tm, tn), lambda i, j: (i, jnp.maximum(j - nq, 0))),
        ],
        out_shape=[jax.ShapeDtypeStruct((m, HR), BF16), jax.ShapeDtypeStruct((m, n - HR), F32)],
        scratch_shapes=[pltpu.VMEM((tm, d), BF16)],
        compiler_params=_params(2),
        name="inproj",
    )(h, gain, w, q_gain)


DSA_TQ = 256
DSA_KC = 256
DSA_RB = 64


def _dsa_kernel(q_ref, iq_ref, ckv_ref, ik_ref, iw_ref, bias_ref, kn_ref, o_ref,
                ckv_s, ik_s, q_s, iq_s, wb_s, sc_s, sct_s, lg_s, m_s, l_s, acc_s, *, topk):
    qi = pl.program_id(1)
    tq, kc, nh, sub, rb = DSA_TQ, DSA_KC, ATT_HEADS, Q_BLOCK, DSA_RB
    nrb = tq // rb
    nchunk = qi + 1

    @pl.when(qi == 0)
    def _():
        ckv_s[...] = _rms(ckv_ref[...], kn_ref[...]).astype(BF16)
        ik_s[...] = ik_ref[:, :IDX_DIM].astype(BF16)

    for h in range(nh):
        q_s[h * tq:(h + 1) * tq, :] = q_ref[:, h * KV_RANK:(h + 1) * KV_RANK]
    w = iw_ref[:, IDX_DIM:IDX_DIM + IDX_HEADS] * (IQW ** -0.5)
    for r in range(nrb):
        for h in range(IDX_HEADS):
            iq_s[r, h * rb:(h + 1) * rb, :] = iq_ref[r * rb:(r + 1) * rb, h * IDX_DIM:(h + 1) * IDX_DIM].astype(BF16)
            wb_s[r, h] = jnp.broadcast_to(w[r * rb:(r + 1) * rb, h:h + 1], (rb, LANE))

    nt_dims = (((1,), (1,)), ((), ()))
    row_rb = lax.broadcasted_iota(jnp.int32, (rb, kc), 0)
    col_rb = lax.broadcasted_iota(jnp.int32, (rb, kc), 1)

    last = nchunk - 1

    def for_chunk_pairs(body, start):
        def pair(p, carry):
            c0 = start + 2 * p
            body(c0)
            body(jnp.minimum(c0 + 1, last))
            return carry

        lax.fori_loop(0, (nchunk - start + 1) // 2, pair, 0)

    def score_chunk(c):
        off = pl.multiple_of(c * kc, kc)
        ikc = ik_s[pl.ds(off, kc), :]
        for r in range(nrb):
            d = lax.dot_general(iq_s[r], ikc, nt_dims, preferred_element_type=F32)
            d = jnp.maximum(d, 0.0).reshape(IDX_HEADS, rb, kc)
            wb = wb_s[r]
            s = jnp.sum(d * jnp.concatenate([wb, wb], axis=-1), axis=0)
            causal = (c * kc + col_rb) <= (qi * tq + r * rb + row_rb)
            sc_s[c, r * rb:(r + 1) * rb, :] = jnp.where(causal, s, -jnp.inf)
        sct_s[c] = sc_s[c].T

    for_chunk_pairs(score_chunk, 0)

    npair = (nchunk + 1) // 2

    @pl.when(nchunk % 2 == 1)
    def _():
        sct_s[nchunk] = jnp.full((kc, tq), -jnp.inf, F32)

    kf = float(topk)

    def count(pred):
        def pair_body(p, acc):
            for c in (2 * p, 2 * p + 1):
                hit = jnp.where(pred(sct_s[c].reshape(kc // 32, 4, SUBLANE, tq)), 1.0, 0.0)
                acc = acc + jnp.sum(jnp.sum(hit, axis=1), axis=0)
            return acc

        acc = lax.fori_loop(0, npair, pair_body, jnp.zeros((SUBLANE, tq), F32))
        return jnp.sum(acc, axis=0, keepdims=True)

    def key_to_float(key):
        return lax.bitcast_convert_type(jnp.where(key >= 0, key, key ^ jnp.int32(0x7FFFFFFF)), F32)

    def bit_body(it, t_key):
        cand = t_key + lax.shift_left(jnp.int32(1), 31 - it)
        cf = key_to_float(cand)
        cnt = count(lambda st: st >= cf[None, None])
        ok = (cnt >= kf) | (cand < KEY_NEG_INF)
        return jnp.where(ok, cand, t_key)

    t_key = lax.fori_loop(0, 32, bit_body, jnp.full((SUBLANE, tq), -2 ** 31, jnp.int32))
    thr_row = jnp.maximum(key_to_float(t_key), F32_LOWEST)

    def to_rows(v):
        return jnp.broadcast_to(v[0:1, :], (kc, tq)).T

    n_gt = count(lambda st: st > thr_row[None, None])
    n_ge = count(lambda st: st >= thr_row[None, None])
    need = kf - n_gt

    @pl.when(jnp.max(n_ge - kf) > 0.0)
    def _():
        kidx_t = lax.broadcasted_iota(jnp.int32, (kc, tq), 0).astype(F32)

        def mark_body(c, carry):
            sct_s[c] = jnp.where(sct_s[c] == thr_row[0:1, :], kidx_t + (c * kc).astype(F32), jnp.inf)
            return carry

        lax.fori_loop(0, 2 * npair, mark_body, 0)

        def ybit_body(it, y):
            cand = y + lax.shift_left(jnp.int32(1), 11 - it)
            cnt = count(lambda tk: tk < cand.astype(F32)[None, None])
            return jnp.where(cnt < need, cand, y)

        y = lax.fori_loop(0, 12, ybit_body, jnp.zeros((SUBLANE, tq), jnp.int32))
        y_rows = to_rows(y.astype(F32))
        thr_rows = to_rows(thr_row)
        kcol = lax.broadcasted_iota(jnp.int32, (tq, kc), 1).astype(F32)

        def drop_body(c, carry):
            s = sc_s[c]
            drop = (s == thr_rows) & ((kcol + (c * kc).astype(F32)) > y_rows)
            sc_s[c] = jnp.where(drop, -jnp.inf, s)
            return carry

        lax.fori_loop(0, nchunk, drop_body, 0)

    thr = to_rows(thr_row)


    def bias_tile(c, a, b):
        return bias_ref[jnp.clip(2 * qi + a - 2 * c - b, 0, 2)]

    def logit_chunk(c, first=False):
        off = pl.multiple_of(c * kc, kc)
        lg = lax.dot_general(q_s[...], ckv_s[pl.ds(off, kc), :], nt_dims, preferred_element_type=F32)
        bias = jnp.concatenate(
            [jnp.concatenate([bias_tile(c, a, 0), bias_tile(c, a, 1)], axis=-1) for a in range(tq // sub)],
            axis=-2)
        keep = sc_s[c] >= thr
        lg = jnp.where(keep[None], lg.reshape(nh, tq, kc) + bias, -jnp.inf)
        lg_s[c] = lg
        mx = jnp.maximum(lg[..., :LANE], lg[..., LANE:])
        m_s[...] = mx if first else jnp.maximum(m_s[...], mx)

    logit_chunk(0, first=True)
    for_chunk_pairs(logit_chunk, 1)

    m = jnp.max(m_s[...], axis=-1, keepdims=True)
    m_s[...] = jnp.broadcast_to(m, m_s.shape)

    def pv_chunk(c, first=False):
        off = pl.multiple_of(c * kc, kc)
        lg = lg_s[c]
        mb = m_s[...]
        p0 = jnp.exp2(lg[..., :LANE] - mb)
        p1 = jnp.exp2(lg[..., LANE:] - mb)
        p = jnp.concatenate([p0, p1], axis=-1).reshape(nh * tq, kc).astype(BF16)
        pv = jnp.dot(p, ckv_s[pl.ds(off, kc), :], preferred_element_type=F32)
        if first:
            l_s[...] = p0 + p1
            acc_s[...] = pv
        else:
            l_s[...] += p0 + p1
            acc_s[...] += pv

    pv_chunk(0, first=True)

    def pv_pair(p, carry):
        pv_chunk(1 + 2 * p)
        pv_chunk(2 + 2 * p)
        return carry

    lax.fori_loop(0, (nchunk - 1) // 2, pv_pair, 0)

    @pl.when(nchunk % 2 == 0)
    def _():
        pv_chunk(last)

    lsum = jnp.sum(l_s[...], axis=-1, keepdims=True)
    for h in range(nh):
        o_ref[:, h * KV_RANK:(h + 1) * KV_RANK] = (acc_s[h * tq:(h + 1) * tq, :] / lsum[h]).astype(o_ref.dtype)


def _dsa(q3, p3, bias_tiles, kv_gain, topk):
    b, s, _ = p3.shape
    tq, kc, sub, rb = DSA_TQ, DSA_KC, Q_BLOCK, DSA_RB
    assert tq == kc and s % tq == 0 and s <= 2 ** 11
    nkc = s // kc
    assert nkc % 2 == 0
    kern = functools.partial(_dsa_kernel, topk=topk)
    return pl.pallas_call(
        kern,
        grid=(b, s // tq),
        in_specs=[
            pl.BlockSpec((None, tq, HR), lambda i, j: (i, j, 0)),
            pl.BlockSpec((None, tq, IQW), lambda i, j: (i, j, OFF_IQ // IQW)),
            pl.BlockSpec((None, s, KV_RANK), lambda i, j: (i, 0, OFF_CKV // KV_RANK)),
            pl.BlockSpec((None, s, LANE), lambda i, j: (i, 0, OFF_IKW // LANE)),
            pl.BlockSpec((None, tq, LANE), lambda i, j: (i, j, OFF_IKW // LANE)),
            pl.BlockSpec((3, ATT_HEADS, sub, sub), lambda i, j: (0, 0, 0, 0)),
            pl.BlockSpec((1, KV_RANK), lambda i, j: (0, 0)),
        ],
        out_specs=pl.BlockSpec((None, tq, HR), lambda i, j: (i, j, 0)),
        out_shape=jax.ShapeDtypeStruct((b, s, HR), BF16),
        scratch_shapes=[
            pltpu.VMEM((s, KV_RANK), BF16),
            pltpu.VMEM((s, IDX_DIM), BF16),
            pltpu.VMEM((ATT_HEADS * tq, KV_RANK), BF16),
            pltpu.VMEM((tq // rb, IDX_HEADS * rb, IDX_DIM), BF16),
            pltpu.VMEM((tq // rb, IDX_HEADS, rb, LANE), F32),
            pltpu.VMEM((nkc, tq, kc), F32),
            pltpu.VMEM((nkc, kc, tq), F32),
            pltpu.VMEM((nkc, ATT_HEADS, tq, kc), F32),
            pltpu.VMEM((ATT_HEADS, tq, LANE), F32),
            pltpu.VMEM((ATT_HEADS, tq, LANE), F32),
            pltpu.VMEM((ATT_HEADS * tq, KV_RANK), F32),
        ],
        compiler_params=_params(2),
        name="dsa",
    )(q3, p3, p3, p3, p3, bias_tiles, kv_gain)


MLSTM_CHUNK = 128


def _mlstm_kernel(mqk_ref, mv_ref, mo_ref, gt_ref, cw_ref, cb_ref, gb_ref, on_ref, o_ref,
                  c_s, n_s, m_s, ext_s):
    lc = MLSTM_CHUNK
    dk, dv = M_QK_DIM, M_V_DIM

    @pl.when(pl.program_id(1) == 0)
    def _():
        c_s[...] = jnp.zeros(c_s.shape, F32)
        n_s[...] = jnp.zeros(n_s.shape, F32)
        m_s[...] = jnp.zeros(m_s.shape, F32)
        ext_s[0:SUBLANE, :] = jnp.zeros((SUBLANE, MQK), F32)

    x = mqk_ref[...]
    ext_s[SUBLANE:SUBLANE + lc, :] = x
    conv = jnp.broadcast_to(cb_ref[...], (lc, MQK))
    for j in range(CONV_WIDTH):
        lo = SUBLANE - (CONV_WIDTH - 1) + j
        conv = conv + cw_ref[j:j + 1, :] * ext_s[lo:lo + lc, :]
    ext_s[0:SUBLANE, :] = x[lc - SUBLANE:lc, :]
    a = conv * _sigmoid(conv)

    gates = gt_ref[...] + gb_ref[...]
    lsg = _log_sigmoid(gates)
    r_i = lax.broadcasted_iota(jnp.int32, (lc, lc), 0)
    c_i = lax.broadcasted_iota(jnp.int32, (lc, lc), 1)
    tril = r_i >= c_i
    hi = lax.Precision.HIGHEST
    bcum_col = jnp.dot(tril.astype(F32), lsg, precision=hi, preferred_element_type=F32)
    gates_t = gates.T
    bcum_row = jnp.dot(lsg.T, (r_i <= c_i).astype(F32), precision=hi, preferred_element_type=F32)

    for h in range(M_HEADS):
        bc = bcum_col[:, M_HEADS + h:M_HEADS + h + 1]
        br = bcum_row[M_HEADS + h:M_HEADS + h + 1, :]
        ic = gates[:, h:h + 1]
        ir = gates_t[h:h + 1, :]
        mst = m_s[h, 0:1, 0:1]
        q = a[:, h * dk:(h + 1) * dk]
        k = a[:, MQK // 2 + h * dk:MQK // 2 + (h + 1) * dk] * (dk ** -0.5)
        v = mv_ref[:, h * dv:(h + 1) * dv]
        qb16, kb16, vb16 = q.astype(BF16), k.astype(BF16), v.astype(BF16)

        dmat = jnp.where(tril, bc - br + ir, -jnp.inf)
        inter = bc + mst
        m_row = jnp.maximum(inter, jnp.max(dmat, axis=-1, keepdims=True))
        w_inter = jnp.exp(inter - m_row)
        qk = lax.dot_general(qb16, kb16, (((1,), (1,)), ((), ())), preferred_element_type=F32)
        qk = qk * jnp.exp(dmat - m_row)
        num = (w_inter * jnp.dot(qb16, c_s[h].astype(BF16), preferred_element_type=F32)
               + jnp.dot(qk.astype(BF16), vb16, preferred_element_type=F32))
        den = (w_inter * jnp.sum(q * n_s[h, 0:1, :], axis=-1, keepdims=True)
               + jnp.sum(qk, axis=-1, keepdims=True))
        hh = num / jnp.maximum(jnp.abs(den), jnp.exp(-m_row))

        b_last = bc[lc - 1:lc, :]
        g_col = b_last - bc + ic
        g_row = b_last - br + ir
        m_new = jnp.maximum(b_last + mst, jnp.max(g_row, axis=-1, keepdims=True))
        decay = jnp.exp(b_last + mst - m_new)
        kw = jnp.exp(g_col - m_new) * k
        c_s[h] = decay * c_s[h] + lax.dot_general(kw.astype(BF16), vb16, (((0,), (0,)), ((), ())),
                                                   preferred_element_type=F32)
        n_s[h, 0:1, :] = decay * n_s[h, 0:1, :] + jnp.sum(kw, axis=0, keepdims=True)
        m_s[h] = jnp.broadcast_to(m_new, (SUBLANE, LANE))

        hn = _rms(hh, on_ref[:, h * dv:(h + 1) * dv])
        o_ref[:, h * dv:(h + 1) * dv] = (hn * _sigmoid(mo_ref[:, h * dv:(h + 1) * dv])).astype(o_ref.dtype)


def _mlstm(p3, conv_w, conv_b, gate_b, out_gain):
    b, s, _ = p3.shape
    lc = MLSTM_CHUNK
    assert s % lc == 0 and lc == LANE
    return pl.pallas_call(
        _mlstm_kernel,
        grid=(b, s // lc),
        in_specs=[
            pl.BlockSpec((None, lc, MQK), lambda i, j: (i, j, OFF_MQK // MQK)),
            pl.BlockSpec((None, lc, MV), lambda i, j: (i, j, OFF_MV // MV)),
            pl.BlockSpec((None, lc, MV), lambda i, j: (i, j, OFF_MO // MV)),
            pl.BlockSpec((None, lc, LANE), lambda i, j: (i, j, OFF_MG // LANE)),
            pl.BlockSpec((CONV_WIDTH, MQK), lambda i, j: (0, 0)),
            pl.BlockSpec((1, MQK), lambda i, j: (0, 0)),
            pl.BlockSpec((1, LANE), lambda i, j: (0, 0)),
            pl.BlockSpec((1, MV), lambda i, j: (0, 0)),
        ],
        out_specs=pl.BlockSpec((None, lc, MV), lambda i, j: (i, j, 0)),
        out_shape=jax.ShapeDtypeStruct((b, s, MV), BF16),
        scratch_shapes=[
            pltpu.VMEM((M_HEADS, M_QK_DIM, M_V_DIM), F32),
            pltpu.VMEM((M_HEADS, SUBLANE, M_QK_DIM), F32),
            pltpu.VMEM((M_HEADS, SUBLANE, LANE), F32),
            pltpu.VMEM((SUBLANE + lc, MQK), F32),
        ],
        compiler_params=_params(2),
        name="mlstm",
    )(p3, p3, p3, p3, conv_w, conv_b, gate_b, out_gain)


def _mergeout_kernel(oa_ref, hm_ref, ga_ref, gm_ref, h_ref, wa_ref, wm_ref, wo_ref, o_ref, mg_s, *, tn):
    d = o_ref.shape[1]
    oa = oa_ref[...]
    hm = hm_ref[...]
    for n in range(d // tn):
        sl = slice(n * tn, (n + 1) * tn)
        ya = jnp.dot(oa, wa_ref[:, sl], preferred_element_type=F32)
        ym = jnp.dot(hm, wm_ref[:, sl], preferred_element_type=F32)
        mg_s[:, sl] = (_sigmoid(ga_ref[:, sl]) * ya + _sigmoid(gm_ref[:, sl]) * ym).astype(BF16)
    mg = mg_s[...]
    for n in range(d // tn):
        sl = slice(n * tn, (n + 1) * tn)
        o_ref[:, sl] = h_ref[:, sl] + jnp.dot(mg, wo_ref[:, sl], preferred_element_type=F32)


def _mergeout(oa, hm, p2, h, wa, wm, wo, l):
    m, d = h.shape
    tm = _tile(m, 256)
    tn = _tile(d, 512)
    assert OFF_GA % d == 0
    ga0 = OFF_GA // d
    resident = functools.partial(pl.BlockSpec, pipeline_mode=pl.Buffered(1))
    return pl.pallas_call(
        functools.partial(_mergeout_kernel, tn=tn),
        grid=(m // tm,),
        in_specs=[
            pl.BlockSpec((tm, HR), lambda i: (i, 0)),
            pl.BlockSpec((tm, MV), lambda i: (i, 0)),
            pl.BlockSpec((tm, d), lambda i: (i, ga0)),
            pl.BlockSpec((tm, d), lambda i: (i, ga0 + 1)),
            pl.BlockSpec((tm, d), lambda i: (i, 0)),
            resident((None, HR, d), lambda i: (l, 0, 0)),
            resident((None, MV, d), lambda i: (l, 0, 0)),
            resident((None, d, d), lambda i: (l, 0, 0)),
        ],
        out_specs=pl.BlockSpec((tm, d), lambda i: (i, 0)),
        out_shape=jax.ShapeDtypeStruct((m, d), F32),
        scratch_shapes=[pltpu.VMEM((tm, d), BF16)],
        compiler_params=_params(1),
        name="mergeout",
    )(oa, hm, p2, p2, h, wa, wm, wo)


def _bias_tiles(rel_bias):
    tq = Q_BLOCK
    i = jnp.arange(tq)[:, None]
    j = jnp.arange(tq)[None, :]

    def bucket(rel):
        n = jnp.maximum(rel, 0)
        max_exact = REL_BUCKETS // 2
        n_large = jnp.maximum(n, max_exact).astype(F32)
        large = max_exact + (jnp.log(n_large / max_exact) / math.log(REL_MAX_DIST / max_exact)
                             * (REL_BUCKETS - max_exact)).astype(jnp.int32)
        large = jnp.minimum(large, REL_BUCKETS - 1)
        return jnp.where(n < max_exact, n, large)

    onehot = jax.nn.one_hot(jnp.stack([bucket(i - j + d * tq) for d in range(3)]), REL_BUCKETS, dtype=F32)
    return jnp.einsum("dijk,kh->dhij", onehot, rel_bias.astype(F32) * LOG2E, precision=lax.Precision.HIGHEST)


def _w_in_kernel(w_ref, o_ref, *, segs):
    for dst, src, width in segs:
        if src is None:
            o_ref[0, dst:dst + width, :] = jnp.zeros((width, o_ref.shape[2]), o_ref.dtype)
        else:
            o_ref[0, dst:dst + width, :] = w_ref[0, src:src + width, :].astype(o_ref.dtype)


def _prep_w_in(w_in, d):
    depth, dm, din = w_in.shape
    widths = (HR, KV_RANK, IQW, IDX_DIM, IDX_HEADS, MQK, MV, MV, M_HEADS, M_HEADS, d, d)
    offs = [0]
    for wdt in widths:
        offs.append(offs[-1] + wdt)
    assert offs[-1] == din
    aq, ckv, iq, ik, _, mqk, _, _, mi, _, ga, _ = offs[:12]
    pieces = [(aq, HR), (iq, IQW), (ckv, KV_RANK), (ik, IDX_DIM + IDX_HEADS), (None, LANE - IDX_DIM - IDX_HEADS),
              (mi, 2 * M_HEADS), (None, LANE - 2 * M_HEADS), (mqk, MQK + 2 * MV), (ga, 2 * d)]
    segs = []
    dst = 0
    for src, width in pieces:
        segs.append((dst, src, width))
        dst += width
    assert all(o % SUBLANE == 0 and (src is None or src % SUBLANE == 0) for o, src, _ in segs)
    tc = _tile(dm, 256)
    return pl.pallas_call(
        functools.partial(_w_in_kernel, segs=tuple(segs)),
        grid=(depth, dm // tc),
        in_specs=[pl.BlockSpec((1, din, tc), lambda l, i: (l, 0, i))],
        out_specs=pl.BlockSpec((1, dst, tc), lambda l, i: (l, 0, i)),
        out_shape=jax.ShapeDtypeStruct((depth, dst, dm), BF16),
        compiler_params=_params(2),
        name="prep_w_in",
    )(jnp.swapaxes(w_in, 1, 2))


def kernel(x, rel_bias, ffn1_norm, ffn1_w_gu, ffn1_w_down, mix_norm, w_in, q_norm, kv_norm, conv_w, conv_b,
           igate_b, fgate_b, m_out_norm, w_att_out, w_mem_out, w_out, ffn2_norm, ffn2_w_gu, ffn2_w_down):
    b, s, d = x.shape
    depth = w_in.shape[0]
    topk = min(TOPK_MAX, s // 4)
    m = b * s
    h = x.reshape(m, d)
    bias_tiles = _bias_tiles(rel_bias)
    w1gu, w2gu = ffn1_w_gu.astype(BF16), ffn2_w_gu.astype(BF16)
    w1d, w2d = ffn1_w_down, ffn2_w_down
    w_in_r = _prep_w_in(w_in, d)
    w_att = w_att_out.astype(BF16).reshape(depth, HR, d)
    w_mem = w_mem_out.astype(BF16)
    w_o = w_out.astype(BF16)
    gate_b = jnp.concatenate([igate_b, fgate_b, jnp.zeros((depth, LANE - 2 * M_HEADS), F32)], axis=1)
    for l in range(depth):
        h = _ffn(h, ffn1_norm[l][None], w1gu, w1d, l)

        q2, p2 = _inproj(h, mix_norm[l][None], w_in_r, q_norm[l][None], l)
        p3 = p2.reshape(b, s, p2.shape[1])
        o_att = _dsa(q2.reshape(b, s, HR), p3, bias_tiles, kv_norm[l][None], topk)
        hm = _mlstm(p3, conv_w[l], conv_b[l][None], gate_b[l][None], m_out_norm[l].reshape(1, MV))
        h = _mergeout(o_att.reshape(m, HR), hm.reshape(m, MV), p2, h, w_att, w_mem, w_o, l)

        h = _ffn(h, ffn2_norm[l][None], w2gu, w2d, l)
    return h.reshape(b, s, d)
```

```python
import functools
import math

import jax
import jax.numpy as jnp
from jax import lax
from jax.experimental import pallas as pl
from jax.experimental.pallas import tpu as pltpu

F32 = jnp.float32
BF16 = jnp.bfloat16

ATT_HEADS = 8
KV_RANK = 256
IDX_HEADS = 8
IDX_DIM = 64
TOPK_MAX = 256
Q_BLOCK = 128
REL_BUCKETS = 32
REL_MAX_DIST = 128
M_HEADS = 4
M_QK_DIM = 128
M_V_DIM = 256
CONV_WIDTH = 4
EPS = 1e-6

HR = ATT_HEADS * KV_RANK
IQW = IDX_HEADS * IDX_DIM
MQK = 2 * M_HEADS * M_QK_DIM
MV = M_HEADS * M_V_DIM
LANE = 128
SUBLANE = 8
VMEM_LIMIT = 56 * 1024 * 1024
FFN_VMEM_LIMIT = 60 * 1024 * 1024

OFF_IQ = 0
OFF_CKV = OFF_IQ + IQW
OFF_IKW = OFF_CKV + KV_RANK
OFF_MG = OFF_IKW + LANE
OFF_MQK = OFF_MG + LANE
OFF_MV = OFF_MQK + MQK
OFF_MO = OFF_MV + MV
OFF_GA = OFF_MO + MV

KEY_NEG_INF = -2139095041
LOG2E = math.log2(math.e)
Q_SCALE = KV_RANK ** -0.5 * LOG2E
F32_LOWEST = float(jnp.finfo(jnp.float32).min)


def _params(n_grid, vmem_limit=VMEM_LIMIT):
    return pltpu.CompilerParams(dimension_semantics=("arbitrary",) * n_grid,
                                vmem_limit_bytes=vmem_limit)


def _tile(n, pref):
    if n <= pref:
        return n
    t = (pref // LANE) * LANE
    while n % t:
        t -= LANE
    return t


def _rms(x, g):
    return x * lax.rsqrt(jnp.mean(x * x, axis=-1, keepdims=True) + EPS) * g


def _sigmoid(x):
    return 1.0 / (1.0 + jnp.exp(-x))


def _log_sigmoid(x):
    return jnp.minimum(x, 0.0) - jnp.log(1.0 + jnp.exp(-jnp.abs(x)))


FFN_NORM_CHUNKS = 4


def _ffn_kernel(h_ref, g_ref, wg_ref, wu_ref, wd_ref, o_ref, xn_ref, *, overlap):
    j = pl.program_id(1)

    def step(first):
        xn = xn_ref[...]
        g = jnp.dot(xn, wg_ref[0], preferred_element_type=F32)
        u = jnp.dot(xn, wu_ref[0], preferred_element_type=F32)
        a = g * _sigmoid(g) * u * 0.5
        if overlap:
            col = lax.broadcasted_iota(jnp.int32, a.shape, 1)
            a = jnp.where((j == pl.num_programs(1) - 1) & (col < overlap), 0.0, a)
        y = jnp.dot(a.astype(BF16), wd_ref[0].astype(BF16), preferred_element_type=F32)
        if first:
            o_ref[...] = h_ref[...] + y
        else:
            o_ref[...] += y

    @pl.when(j == 0)
    def _():
        rows = h_ref.shape[0] // FFN_NORM_CHUNKS
        for r in range(FFN_NORM_CHUNKS):
            sl = slice(r * rows, (r + 1) * rows)
            xn_ref[sl, :] = _rms(h_ref[sl, :], g_ref[...]).astype(BF16)
        step(True)

    @pl.when(j > 0)
    def _():
        step(False)


def _ffn(h, gain, w_gu, w_down, l):
    m, d = h.shape
    f = w_down.shape[1]
    tm = _tile(m, 1024)
    tf = min(f, 512)
    nj = -(-f // tf)
    assert f % LANE == 0 and tf % LANE == 0

    def off(j):
        return pl.multiple_of(jnp.minimum(j * tf, f - tf), LANE)

    return pl.pallas_call(
        functools.partial(_ffn_kernel, overlap=nj * tf - f),
        grid=(m // tm, nj),
        in_specs=[
            pl.BlockSpec((tm, d), lambda i, j: (i, 0)),
            pl.BlockSpec((1, d), lambda i, j: (0, 0)),
            pl.BlockSpec((pl.Element(1), pl.Element(d), pl.Element(tf)), lambda i, j: (l, 0, off(j))),
            pl.BlockSpec((pl.Element(1), pl.Element(d), pl.Element(tf)),
                         lambda i, j: (l, 0, pl.multiple_of(f + off(j), LANE))),
            pl.BlockSpec((pl.Element(1), pl.Element(tf), pl.Element(d)), lambda i, j: (l, off(j), 0)),
        ],
        out_specs=pl.BlockSpec((tm, d), lambda i, j: (i, 0)),
        out_shape=jax.ShapeDtypeStruct((m, d), F32),
        scratch_shapes=[pltpu.VMEM((tm, d), BF16)],
        compiler_params=_params(2, FFN_VMEM_LIMIT),
        name="ffn",
    )(h, gain, w_gu, w_gu, w_down)


def _inproj_kernel(h_ref, g_ref, w_ref, qn_ref, q_ref, p_ref, xn_ref, *, nq):
    j = pl.program_id(1)
    nt_dims = (((1,), (1,)), ((), ()))

    @pl.when(j == 0)
    def _():
        xn_ref[...] = _rms(h_ref[...], g_ref[...]).astype(BF16)

    @pl.when(j < nq)
    def _():
        acc = lax.dot_general(xn_ref[...], w_ref[...], nt_dims, preferred_element_type=F32)
        for k in range(acc.shape[1] // KV_RANK):
            sl = slice(k * KV_RANK, (k + 1) * KV_RANK)
            q_ref[:, sl] = (_rms(acc[:, sl], qn_ref[...]) * Q_SCALE).astype(q_ref.dtype)

    @pl.when(j >= nq)
    def _():
        p_ref[...] = lax.dot_general(xn_ref[...], w_ref[...], nt_dims, preferred_element_type=F32)


def _inproj(h, gain, w, q_gain, l):
    m, d = h.shape
    n = w.shape[1]
    tm = _tile(m, 1024)
    tn = 1024 if n % 1024 == 0 else 512
    assert HR % tn == 0 and n % tn == 0 and tn % KV_RANK == 0
    nq = HR // tn
    return pl.pallas_call(
        functools.partial(_inproj_kernel, nq=nq),
        grid=(m // tm, n // tn),
        in_specs=[
            pl.BlockSpec((tm, d), lambda i, j: (i, 0)),
            pl.BlockSpec((1, d), lambda i, j: (0, 0)),
            pl.BlockSpec((None, tn, d), lambda i, j: (l, j, 0)),
            pl.BlockSpec((1, KV_RANK), lambda i, j: (0, 0)),
        ],
        out_specs=[
            pl.BlockSpec((tm, tn), lambda i, j: (i, jnp.minimum(j, nq - 1))),
            pl.BlockSpec((tm, tn), lambda i, j: (i, jnp.maximum(j - nq, 0))),
        ],
        out_shape=[jax.ShapeDtypeStruct((m, HR), BF16), jax.ShapeDtypeStruct((m, n - HR), F32)],
        scratch_shapes=[pltpu.VMEM((tm, d), BF16)],
        compiler_params=_params(2),
        name="inproj",
    )(h, gain, w, q_gain)


DSA_TQ = 256
DSA_KC = 256
DSA_RB = 64


def _dsa_kernel(q_ref, iq_ref, ckv_ref, ik_ref, iw_ref, bias_ref, kn_ref, o_ref,
                ckv_s, ik_s, q_s, iq_s, wb_s, sc_s, sct_s, lg_s, m_s, l_s, acc_s, *, topk):
    qi = pl.program_id(1)
    tq, kc, nh, sub, rb = DSA_TQ, DSA_KC, ATT_HEADS, Q_BLOCK, DSA_RB
    nrb = tq // rb
    nchunk = qi + 1

    @pl.when(qi == 0)
    def _():
        ckv_s[...] = _rms(ckv_ref[...], kn_ref[...]).astype(BF16)
        ik_s[...] = ik_ref[:, :IDX_DIM].astype(BF16)

    for h in range(nh):
        q_s[h * tq:(h + 1) * tq, :] = q_ref[:, h * KV_RANK:(h + 1) * KV_RANK]
    w = iw_ref[:, IDX_DIM:IDX_DIM + IDX_HEADS] * (IQW ** -0.5)
    for r in range(nrb):
        for h in range(IDX_HEADS):
            iq_s[r, h * rb:(h + 1) * rb, :] = iq_ref[r * rb:(r + 1) * rb, h * IDX_DIM:(h + 1) * IDX_DIM].astype(BF16)
            wb_s[r, h] = jnp.broadcast_to(w[r * rb:(r + 1) * rb, h:h + 1], (rb, LANE))

    nt_dims = (((1,), (1,)), ((), ()))
    row_rb = lax.broadcasted_iota(jnp.int32, (rb, kc), 0)
    col_rb = lax.broadcasted_iota(jnp.int32, (rb, kc), 1)

    last = nchunk - 1

    def for_chunk_pairs(body, start):
        def pair(p, carry):
            c0 = start + 2 * p
            body(c0)
            body(jnp.minimum(c0 + 1, last))
            return carry

        lax.fori_loop(0, (nchunk - start + 1) // 2, pair, 0)

    def score_chunk(c):
        off = pl.multiple_of(c * kc, kc)
        ikc = ik_s[pl.ds(off, kc), :]
        for r in range(nrb):
            d = lax.dot_general(iq_s[r], ikc, nt_dims, preferred_element_type=F32)
            d = jnp.maximum(d, 0.0).reshape(IDX_HEADS, rb, kc)
            wb = wb_s[r]
            s = jnp.sum(d * jnp.concatenate([wb, wb], axis=-1), axis=0)
            causal = (c * kc + col_rb) <= (qi * tq + r * rb + row_rb)
            sc_s[c, r * rb:(r + 1) * rb, :] = jnp.where(causal, s, -jnp.inf)
        sct_s[c] = sc_s[c].T

    for_chunk_pairs(score_chunk, 0)

    npair = (nchunk + 1) // 2

    @pl.when(nchunk % 2 == 1)
    def _():
        sct_s[nchunk] = jnp.full((kc, tq), -jnp.inf, F32)

    kf = float(topk)

    def count(pred):
        def pair_body(p, acc):
            for c in (2 * p, 2 * p + 1):
                hit = jnp.where(pred(sct_s[c].reshape(kc // 32, 4, SUBLANE, tq)), 1.0, 0.0)
                acc = acc + jnp.sum(jnp.sum(hit, axis=1), axis=0)
            return acc

        acc = lax.fori_loop(0, npair, pair_body, jnp.zeros((SUBLANE, tq), F32))
        return jnp.sum(acc, axis=0, keepdims=True)

    def key_to_float(key):
        return lax.bitcast_convert_type(jnp.where(key >= 0, key, key ^ jnp.int32(0x7FFFFFFF)), F32)

    def bit_body(it, t_key):
        cand = t_key + lax.shift_left(jnp.int32(1), 31 - it)
        cf = key_to_float(cand)
        cnt = count(lambda st: st >= cf[None, None])
        ok = (cnt >= kf) | (cand < KEY_NEG_INF)
        return jnp.where(ok, cand, t_key)

    t_key = lax.fori_loop(0, 32, bit_body, jnp.full((SUBLANE, tq), -2 ** 31, jnp.int32))
    thr_row = jnp.maximum(key_to_float(t_key), F32_LOWEST)

    def to_rows(v):
        return jnp.broadcast_to(v[0:1, :], (kc, tq)).T

    n_gt = count(lambda st: st > thr_row[None, None])
    n_ge = count(lambda st: st >= thr_row[None, None])
    need = kf - n_gt

    @pl.when(jnp.max(n_ge - kf) > 0.0)
    def _():
        kidx_t = lax.broadcasted_iota(jnp.int32, (kc, tq), 0).astype(F32)

        def mark_body(c, carry):
            sct_s[c] = jnp.where(sct_s[c] == thr_row[0:1, :], kidx_t + (c * kc).astype(F32), jnp.inf)
            return carry

        lax.fori_loop(0, 2 * npair, mark_body, 0)

        def ybit_body(it, y):
            cand = y + lax.shift_left(jnp.int32(1), 11 - it)
            cnt = count(lambda tk: tk < cand.astype(F32)[None, None])
            return jnp.where(cnt < need, cand, y)

        y = lax.fori_loop(0, 12, ybit_body, jnp.zeros((SUBLANE, tq), jnp.int32))
        y_rows = to_rows(y.astype(F32))
        thr_rows = to_rows(thr_row)
        kcol = lax.broadcasted_iota(jnp.int32, (tq, kc), 1).astype(F32)

        def drop_body(c, carry):
            s = sc_s[c]
            drop = (s == thr_rows) & ((kcol + (c * kc).astype(F32)) > y_rows)
            sc_s[c] = jnp.where(drop, -jnp.inf, s)
            return carry

        lax.fori_loop(0, nchunk, drop_body, 0)

    thr = to_rows(thr_row)


    def bias_tile(c, a, b):
        return bias_ref[jnp.clip(2 * qi + a - 2 * c - b, 0, 2)]

    def logit_chunk(c, first=False):
        off = pl.multiple_of(c * kc, kc)
        lg = lax.dot_general(q_s[...], ckv_s[pl.ds(off, kc), :], nt_dims, preferred_element_type=F32)
        bias = jnp.concatenate(
            [jnp.concatenate([bias_tile(c, a, 0), bias_tile(c, a, 1)], axis=-1) for a in range(tq // sub)],
            axis=-2)
        keep = sc_s[c] >= thr
        lg = jnp.where(keep[None], lg.reshape(nh, tq, kc) + bias, -jnp.inf)
        lg_s[c] = lg
        mx = jnp.maximum(lg[..., :LANE], lg[..., LANE:])
        m_s[...] = mx if first else jnp.maximum(m_s[...], mx)

    logit_chunk(0, first=True)
    for_chunk_pairs(logit_chunk, 1)

    m = jnp.max(m_s[...], axis=-1, keepdims=True)
    m_s[...] = jnp.broadcast_to(m, m_s.shape)

    def pv_chunk(c, first=False):
        off = pl.multiple_of(c * kc, kc)
        lg = lg_s[c]
        mb = m_s[...]
        p0 = jnp.exp2(lg[..., :LANE] - mb)
        p1 = jnp.exp2(lg[..., LANE:] - mb)
        p = jnp.concatenate([p0, p1], axis=-1).reshape(nh * tq, kc).astype(BF16)
        pv = jnp.dot(p, ckv_s[pl.ds(off, kc), :], preferred_element_type=F32)
        if first:
            l_s[...] = p0 + p1
            acc_s[...] = pv
        else:
            l_s[...] += p0 + p1
            acc_s[...] += pv

    pv_chunk(0, first=True)

    def pv_pair(p, carry):
        pv_chunk(1 + 2 * p)
        pv_chunk(2 + 2 * p)
        return carry

    lax.fori_loop(0, (nchunk - 1) // 2, pv_pair, 0)

    @pl.when(nchunk % 2 == 0)
    def _():
        pv_chunk(last)

    lsum = jnp.sum(l_s[...], axis=-1, keepdims=True)
    for h in range(nh):
        o_ref[:, h * KV_RANK:(h + 1) * KV_RANK] = (acc_s[h * tq:(h + 1) * tq, :] / lsum[h]).astype(o_ref.dtype)


def _dsa(q3, p3, bias_tiles, kv_gain, topk):
    b, s, _ = p3.shape
    tq, kc, sub, rb = DSA_TQ, DSA_KC, Q_BLOCK, DSA_RB
    assert tq == kc and s % tq == 0 and s <= 2 ** 11
    nkc = s // kc
    assert nkc % 2 == 0
    kern = functools.partial(_dsa_kernel, topk=topk)
    return pl.pallas_call(
        kern,
        grid=(b, s // tq),
        in_specs=[
            pl.BlockSpec((None, tq, HR), lambda i, j: (i, j, 0)),
            pl.BlockSpec((None, tq, IQW), lambda i, j: (i, j, OFF_IQ // IQW)),
            pl.BlockSpec((None, s, KV_RANK), lambda i, j: (i, 0, OFF_CKV // KV_RANK)),
            pl.BlockSpec((None, s, LANE), lambda i, j: (i, 0, OFF_IKW // LANE)),
            pl.BlockSpec((None, tq, LANE), lambda i, j: (i, j, OFF_IKW // LANE)),
            pl.BlockSpec((3, ATT_HEADS, sub, sub), lambda i, j: (0, 0, 0, 0)),
            pl.BlockSpec((1, KV_RANK), lambda i, j: (0, 0)),
        ],
        out_specs=pl.BlockSpec((None, tq, HR), lambda i, j: (i, j, 0)),
        out_shape=jax.ShapeDtypeStruct((b, s, HR), BF16),
        scratch_shapes=[
            pltpu.VMEM((s, KV_RANK), BF16),
            pltpu.VMEM((s, IDX_DIM), BF16),
            pltpu.VMEM((ATT_HEADS * tq, KV_RANK), BF16),
            pltpu.VMEM((tq // rb, IDX_HEADS * rb, IDX_DIM), BF16),
            pltpu.VMEM((tq // rb, IDX_HEADS, rb, LANE), F32),
            pltpu.VMEM((nkc, tq, kc), F32),
            pltpu.VMEM((nkc, kc, tq), F32),
            pltpu.VMEM((nkc, ATT_HEADS, tq, kc), F32),
            pltpu.VMEM((ATT_HEADS, tq, LANE), F32),
            pltpu.VMEM((ATT_HEADS, tq, LANE), F32),
            pltpu.VMEM((ATT_HEADS * tq, KV_RANK), F32),
        ],
        compiler_params=_params(2),
        name="dsa",
    )(q3, p3, p3, p3, p3, bias_tiles, kv_gain)


MLSTM_CHUNK = 128


def _mlstm_kernel(mqk_ref, mv_ref, mo_ref, gt_ref, cw_ref, cb_ref, gb_ref, on_ref, o_ref,
                  c_s, n_s, m_s, ext_s):
    lc = MLSTM_CHUNK
    dk, dv = M_QK_DIM, M_V_DIM

    @pl.when(pl.program_id(1) == 0)
    def _():
        c_s[...] = jnp.zeros(c_s.shape, F32)
        n_s[...] = jnp.zeros(n_s.shape, F32)
        m_s[...] = jnp.zeros(m_s.shape, F32)
        ext_s[0:SUBLANE, :] = jnp.zeros((SUBLANE, MQK), F32)

    x = mqk_ref[...]
    ext_s[SUBLANE:SUBLANE + lc, :] = x
    conv = jnp.broadcast_to(cb_ref[...], (lc, MQK))
    for j in range(CONV_WIDTH):
        lo = SUBLANE - (CONV_WIDTH - 1) + j
        conv = conv + cw_ref[j:j + 1, :] * ext_s[lo:lo + lc, :]
    ext_s[0:SUBLANE, :] = x[lc - SUBLANE:lc, :]
    a = conv * _sigmoid(conv)

    gates = gt_ref[...] + gb_ref[...]
    lsg = _log_sigmoid(gates)
    r_i = lax.broadcasted_iota(jnp.int32, (lc, lc), 0)
    c_i = lax.broadcasted_iota(jnp.int32, (lc, lc), 1)
    tril = r_i >= c_i
    hi = lax.Precision.HIGHEST
    bcum_col = jnp.dot(tril.astype(F32), lsg, precision=hi, preferred_element_type=F32)
    gates_t = gates.T
    bcum_row = jnp.dot(lsg.T, (r_i <= c_i).astype(F32), precision=hi, preferred_element_type=F32)

    for h in range(M_HEADS):
        bc = bcum_col[:, M_HEADS + h:M_HEADS + h + 1]
        br = bcum_row[M_HEADS + h:M_HEADS + h + 1, :]
        ic = gates[:, h:h + 1]
        ir = gates_t[h:h + 1, :]
        mst = m_s[h, 0:1, 0:1]
        q = a[:, h * dk:(h + 1) * dk]
        k = a[:, MQK // 2 + h * dk:MQK // 2 + (h + 1) * dk] * (dk ** -0.5)
        v = mv_ref[:, h * dv:(h + 1) * dv]
        qb16, kb16, vb16 = q.astype(BF16), k.astype(BF16), v.astype(BF16)

        dmat = jnp.where(tril, bc - br + ir, -jnp.inf)
        inter = bc + mst
        m_row = jnp.maximum(inter, jnp.max(dmat, axis=-1, keepdims=True))
        w_inter = jnp.exp(inter - m_row)
        qk = lax.dot_general(qb16, kb16, (((1,), (1,)), ((), ())), preferred_element_type=F32)
        qk = qk * jnp.exp(dmat - m_row)
        num = (w_inter * jnp.dot(qb16, c_s[h].astype(BF16), preferred_element_type=F32)
               + jnp.dot(qk.astype(BF16), vb16, preferred_element_type=F32))
        den = (w_inter * jnp.sum(q * n_s[h, 0:1, :], axis=-1, keepdims=True)
               + jnp.sum(qk, axis=-1, keepdims=True))
        hh = num / jnp.maximum(jnp.abs(den), jnp.exp(-m_row))

        b_last = bc[lc - 1:lc, :]
        g_col = b_last - bc + ic
        g_row = b_last - br + ir
        m_new = jnp.maximum(b_last + mst, jnp.max(g_row, axis=-1, keepdims=True))
        decay = jnp.exp(b_last + mst - m_new)
        kw = jnp.exp(g_col - m_new) * k
        c_s[h] = decay * c_s[h] + lax.dot_general(kw.astype(BF16), vb16, (((0,), (0,)), ((), ())),
                                                   preferred_element_type=F32)
        n_s[h, 0:1, :] = decay * n_s[h, 0:1, :] + jnp.sum(kw, axis=0, keepdims=True)
        m_s[h] = jnp.broadcast_to(m_new, (SUBLANE, LANE))

        hn = _rms(hh, on_ref[:, h * dv:(h + 1) * dv])
        o_ref[:, h * dv:(h + 1) * dv] = (hn * _sigmoid(mo_ref[:, h * dv:(h + 1) * dv])).astype(o_ref.dtype)


def _mlstm(p3, conv_w, conv_b, gate_b, out_gain):
    b, s, _ = p3.shape
    lc = MLSTM_CHUNK
    assert s % lc == 0 and lc == LANE
    return pl.pallas_call(
        _mlstm_kernel,
        grid=(b, s // lc),
        in_specs=[
            pl.BlockSpec((None, lc, MQK), lambda i, j: (i, j, OFF_MQK // MQK)),
            pl.BlockSpec((None, lc, MV), lambda i, j: (i, j, OFF_MV // MV)),
            pl.BlockSpec((None, lc, MV), lambda i, j: (i, j, OFF_MO // MV)),
            pl.BlockSpec((None, lc, LANE), lambda i, j: (i, j, OFF_MG // LANE)),
            pl.BlockSpec((CONV_WIDTH, MQK), lambda i, j: (0, 0)),
            pl.BlockSpec((1, MQK), lambda i, j: (0, 0)),
            pl.BlockSpec((1, LANE), lambda i, j: (0, 0)),
            pl.BlockSpec((1, MV), lambda i, j: (0, 0)),
        ],
        out_specs=pl.BlockSpec((None, lc, MV), lambda i, j: (i, j, 0)),
        out_shape=jax.ShapeDtypeStruct((b, s, MV), BF16),
        scratch_shapes=[
            pltpu.VMEM((M_HEADS, M_QK_DIM, M_V_DIM), F32),
            pltpu.VMEM((M_HEADS, SUBLANE, M_QK_DIM), F32),
            pltpu.VMEM((M_HEADS, SUBLANE, LANE), F32),
            pltpu.VMEM((SUBLANE + lc, MQK), F32),
        ],
        compiler_params=_params(2),
        name="mlstm",
    )(p3, p3, p3, p3, conv_w, conv_b, gate_b, out_gain)


def _mergeout_kernel(oa_ref, hm_ref, ga_ref, gm_ref, h_ref, wa_ref, wm_ref, wo_ref, o_ref, mg_s, *, tn):
    d = o_ref.shape[1]
    oa = oa_ref[...]
    hm = hm_ref[...]
    for n in range(d // tn):
        sl = slice(n * tn, (n + 1) * tn)
        ya = jnp.dot(oa, wa_ref[:, sl], preferred_element_type=F32)
        ym = jnp.dot(hm, wm_ref[:, sl], preferred_element_type=F32)
        mg_s[:, sl] = (_sigmoid(ga_ref[:, sl]) * ya + _sigmoid(gm_ref[:, sl]) * ym).astype(BF16)
    mg = mg_s[...]
    for n in range(d // tn):
        sl = slice(n * tn, (n + 1) * tn)
        o_ref[:, sl] = h_ref[:, sl] + jnp.dot(mg, wo_ref[:, sl], preferred_element_type=F32)


def _mergeout(oa, hm, p2, h, wa, wm, wo, l):
    m, d = h.shape
    tm = _tile(m, 256)
    tn = _tile(d, 512)
    assert OFF_GA % d == 0
    ga0 = OFF_GA // d
    resident = functools.partial(pl.BlockSpec, pipeline_mode=pl.Buffered(1))
    return pl.pallas_call(
        functools.partial(_mergeout_kernel, tn=tn),
        grid=(m // tm,),
        in_specs=[
            pl.BlockSpec((tm, HR), lambda i: (i, 0)),
            pl.BlockSpec((tm, MV), lambda i: (i, 0)),
            pl.BlockSpec((tm, d), lambda i: (i, ga0)),
            pl.BlockSpec((tm, d), lambda i: (i, ga0 + 1)),
            pl.BlockSpec((tm, d), lambda i: (i, 0)),
            resident((None, HR, d), lambda i: (l, 0, 0)),
            resident((None, MV, d), lambda i: (l, 0, 0)),
            resident((None, d, d), lambda i: (l, 0, 0)),
        ],
        out_specs=pl.BlockSpec((tm, d), lambda i: (i, 0)),
        out_shape=jax.ShapeDtypeStruct((m, d), F32),
        scratch_shapes=[pltpu.VMEM((tm, d), BF16)],
        compiler_params=_params(1),
        name="mergeout",
    )(oa, hm, p2, p2, h, wa, wm, wo)


def _bias_tiles(rel_bias):
    tq = Q_BLOCK
    i = jnp.arange(tq)[:, None]
    j = jnp.arange(tq)[None, :]

    def bucket(rel):
        n = jnp.maximum(rel, 0)
        max_exact = REL_BUCKETS // 2
        n_large = jnp.maximum(n, max_exact).astype(F32)
        large = max_exact + (jnp.log(n_large / max_exact) / math.log(REL_MAX_DIST / max_exact)
                             * (REL_BUCKETS - max_exact)).astype(jnp.int32)
        large = jnp.minimum(large, REL_BUCKETS - 1)
        return jnp.where(n < max_exact, n, large)

    onehot = jax.nn.one_hot(jnp.stack([bucket(i - j + d * tq) for d in range(3)]), REL_BUCKETS, dtype=F32)
    return jnp.einsum("dijk,kh->dhij", onehot, rel_bias.astype(F32) * LOG2E, precision=lax.Precision.HIGHEST)


def _w_in_kernel(w_ref, o_ref, *, segs):
    for dst, src, width in segs:
        if src is None:
            o_ref[0, dst:dst + width, :] = jnp.zeros((width, o_ref.shape[2]), o_ref.dtype)
        else:
            o_ref[0, dst:dst + width, :] = w_ref[0, src:src + width, :].astype(o_ref.dtype)


def _prep_w_in(w_in, d):
    depth, dm, din = w_in.shape
    widths = (HR, KV_RANK, IQW, IDX_DIM, IDX_HEADS, MQK, MV, MV, M_HEADS, M_HEADS, d, d)
    offs = [0]
    for wdt in widths:
        offs.append(offs[-1] + wdt)
    assert offs[-1] == din
    aq, ckv, iq, ik, _, mqk, _, _, mi, _, ga, _ = offs[:12]
    pieces = [(aq, HR), (iq, IQW), (ckv, KV_RANK), (ik, IDX_DIM + IDX_HEADS), (None, LANE - IDX_DIM - IDX_HEADS),
              (mi, 2 * M_HEADS), (None, LANE - 2 * M_HEADS), (mqk, MQK + 2 * MV), (ga, 2 * d)]
    segs = []
    dst = 0
    for src, width in pieces:
        segs.append((dst, src, width))
        dst += width
    assert all(o % SUBLANE == 0 and (src is None or src % SUBLANE == 0) for o, src, _ in segs)
    tc = _tile(dm, 256)
    return pl.pallas_call(
        functools.partial(_w_in_kernel, segs=tuple(segs)),
        grid=(depth, dm // tc),
        in_specs=[pl.BlockSpec((1, din, tc), lambda l, i: (l, 0, i))],
        out_specs=pl.BlockSpec((1, dst, tc), lambda l, i: (l, 0, i)),
        out_shape=jax.ShapeDtypeStruct((depth, dst, dm), BF16),
        compiler_params=_params(2),
        name="prep_w_in",
    )(jnp.swapaxes(w_in, 1, 2))


def kernel(x, rel_bias, ffn1_norm, ffn1_w_gu, ffn1_w_down, mix_norm, w_in, q_norm, kv_norm, conv_w, conv_b,
           igate_b, fgate_b, m_out_norm, w_att_out, w_mem_out, w_out, ffn2_norm, ffn2_w_gu, ffn2_w_down):
    b, s, d = x.shape
    depth = w_in.shape[0]
    topk = min(TOPK_MAX, s // 4)
    m = b * s
    h = x.reshape(m, d)
    bias_tiles = _bias_tiles(rel_bias)
    w1gu, w2gu = ffn1_w_gu.astype(BF16), ffn2_w_gu.astype(BF16)
    w1d, w2d = ffn1_w_down, ffn2_w_down
    w_in_r = _prep_w_in(w_in, d)
    w_att = w_att_out.astype(BF16).reshape(depth, HR, d)
    w_mem = w_mem_out.astype(BF16)
    w_o = w_out.astype(BF16)
    gate_b = jnp.concatenate([igate_b, fgate_b, jnp.zeros((depth, LANE - 2 * M_HEADS), F32)], axis=1)
    for l in range(depth):
        h = _ffn(h, ffn1_norm[l][None], w1gu, w1d, l)

        q2, p2 = _inproj(h, mix_norm[l][None], w_in_r, q_norm[l][None], l)
        p3 = p2.reshape(b, s, p2.shape[1])
        o_att = _dsa(q2.reshape(b, s, HR), p3, bias_tiles, kv_norm[l][None], topk)
        hm = _mlstm(p3, conv_w[l], conv_b[l][None], gate_b[l][None], m_out_norm[l].reshape(1, MV))
        h = _mergeout(o_att.reshape(m, HR), hm.reshape(m, MV), p2, h, w_att, w_mem, w_o, l)

        h = _ffn(h, ffn2_norm[l][None], w2gu, w2d, l)
    return h.reshape(b, s, d)
```

```python
import functools
import math

import jax
import jax.numpy as jnp
from jax import lax
from jax.experimental import pallas as pl
from jax.experimental.pallas import tpu as pltpu

F32 = jnp.float32
BF16 = jnp.bfloat16

ATT_HEADS = 8
KV_RANK = 256
IDX_HEADS = 8
IDX_DIM = 64
TOPK_MAX = 256
Q_BLOCK = 128
REL_BUCKETS = 32
REL_MAX_DIST = 128
M_HEADS = 4
M_QK_DIM = 128
M_V_DIM = 256
CONV_WIDTH = 4
EPS = 1e-6

HR = ATT_HEADS * KV_RANK
IQW = IDX_HEADS * IDX_DIM
MQK = 2 * M_HEADS * M_QK_DIM
MV = M_HEADS * M_V_DIM
LANE = 128
SUBLANE = 8
VMEM_LIMIT = 56 * 1024 * 1024
FFN_VMEM_LIMIT = 60 * 1024 * 1024

OFF_IQ = 0
OFF_CKV = OFF_IQ + IQW
OFF_IKW = OFF_CKV + KV_RANK
OFF_MG = OFF_IKW + LANE
OFF_MQK = OFF_MG + LANE
OFF_MV = OFF_MQK + MQK
OFF_MO = OFF_MV + MV
OFF_GA = OFF_MO + MV

KEY_NEG_INF = -2139095041
LOG2E = math.log2(math.e)
Q_SCALE = KV_RANK ** -0.5 * LOG2E
F32_LOWEST = float(jnp.finfo(jnp.float32).min)


def _params(n_grid, vmem_limit=VMEM_LIMIT):
    return pltpu.CompilerParams(dimension_semantics=("arbitrary",) * n_grid,
                                vmem_limit_bytes=vmem_limit)


def _tile(n, pref):
    if n <= pref:
        return n
    t = (pref // LANE) * LANE
    while n % t:
        t -= LANE
    return t


def _rms(x, g):
    return x * lax.rsqrt(jnp.mean(x * x, axis=-1, keepdims=True) + EPS) * g


def _sigmoid(x):
    return 1.0 / (1.0 + jnp.exp(-x))


def _log_sigmoid(x):
    return jnp.minimum(x, 0.0) - jnp.log(1.0 + jnp.exp(-jnp.abs(x)))


FFN_NORM_CHUNKS = 4


def _ffn_kernel(h_ref, g_ref, wg_ref, wu_ref, wd_ref, o_ref, xn_ref, *, overlap):
    j = pl.program_id(1)

    def step(first):
        xn = xn_ref[...]
        g = jnp.dot(xn, wg_ref[0], preferred_element_type=F32)
        u = jnp.dot(xn, wu_ref[0], preferred_element_type=F32)
        a = g * _sigmoid(g) * u * 0.5
        if overlap:
            col = lax.broadcasted_iota(jnp.int32, a.shape, 1)
            a = jnp.where((j == pl.num_programs(1) - 1) & (col < overlap), 0.0, a)
        y = jnp.dot(a.astype(BF16), wd_ref[0].astype(BF16), preferred_element_type=F32)
        if first:
            o_ref[...] = h_ref[...] + y
        else:
            o_ref[...] += y

    @pl.when(j == 0)
    def _():
        rows = h_ref.shape[0] // FFN_NORM_CHUNKS
        for r in range(FFN_NORM_CHUNKS):
            sl = slice(r * rows, (r + 1) * rows)
            xn_ref[sl, :] = _rms(h_ref[sl, :], g_ref[...]).astype(BF16)
        step(True)

    @pl.when(j > 0)
    def _():
        step(False)


def _ffn(h, gain, w_gu, w_down, l):
    m, d = h.shape
    f = w_down.shape[1]
    tm = _tile(m, 1024)
    tf = min(f, 512)
    nj = -(-f // tf)
    assert f % LANE == 0 and tf % LANE == 0

    def off(j):
        return pl.multiple_of(jnp.minimum(j * tf, f - tf), LANE)

    return pl.pallas_call(
        functools.partial(_ffn_kernel, overlap=nj * tf - f),
        grid=(m // tm, nj),
        in_specs=[
            pl.BlockSpec((tm, d), lambda i, j: (i, 0)),
            pl.BlockSpec((1, d), lambda i, j: (0, 0)),
            pl.BlockSpec((pl.Element(1), pl.Element(d), pl.Element(tf)), lambda i, j: (l, 0, off(j))),
            pl.BlockSpec((pl.Element(1), pl.Element(d), pl.Element(tf)),
                         lambda i, j: (l, 0, pl.multiple_of(f + off(j), LANE))),
            pl.BlockSpec((pl.Element(1), pl.Element(tf), pl.Element(d)), lambda i, j: (l, off(j), 0)),
        ],
        out_specs=pl.BlockSpec((tm, d), lambda i, j: (i, 0)),
        out_shape=jax.ShapeDtypeStruct((m, d), F32),
        scratch_shapes=[pltpu.VMEM((tm, d), BF16)],
        compiler_params=_params(2, FFN_VMEM_LIMIT),
        name="ffn",
    )(h, gain, w_gu, w_gu, w_down)


def _inproj_kernel(h_ref, g_ref, w_ref, qn_ref, q_ref, p_ref, xn_ref, *, nq):
    j = pl.program_id(1)
    nt_dims = (((1,), (1,)), ((), ()))

    def q_step():
        acc = lax.dot_general(xn_ref[...], w_ref[...], nt_dims, preferred_element_type=F32)
        for k in range(acc.shape[1] // KV_RANK):
            sl = slice(k * KV_RANK, (k + 1) * KV_RANK)
            q_ref[:, sl] = (_rms(acc[:, sl], qn_ref[...]) * Q_SCALE).astype(q_ref.dtype)

    @pl.when(j == 0)
    def _():
        rows = h_ref.shape[0] // FFN_NORM_CHUNKS
        for r in range(FFN_NORM_CHUNKS):
            sl = slice(r * rows, (r + 1) * rows)
            xn_ref[sl, :] = _rms(h_ref[sl, :], g_ref[...]).astype(BF16)
        q_step()

    @pl.when((j > 0) & (j < nq))
    def _():
        q_step()

    @pl.when(j >= nq)
    def _():
        p_ref[...] = lax.dot_general(xn_ref[...], w_ref[...], nt_dims, preferred_element_type=F32)


def _inproj(h, gain, w, q_gain, l):
    m, d = h.shape
    n = w.shape[1]
    tm = _tile(m, 1024)
    tn = 1024 if n % 1024 == 0 else 512
    assert HR % tn == 0 and n % tn == 0 and tn % KV_RANK == 0
    nq = HR // tn
    return pl.pallas_call(
        functools.partial(_inproj_kernel, nq=nq),
        grid=(m // tm, n // tn),
        in_specs=[
            pl.BlockSpec((tm, d), lambda i, j: (i, 0)),
            pl.BlockSpec((1, d), lambda i, j: (0, 0)),
            pl.BlockSpec((None, tn, d), lambda i, j: (l, j, 0)),
            pl.BlockSpec((1, KV_RANK), lambda i, j: (0, 0)),
        ],
        out_specs=[
            pl.BlockSpec((tm, tn), lambda i, j: (i, jnp.minimum(j, nq - 1))),
            pl.BlockSpec((tm, tn), lambda i, j: (i, jnp.maximum(j - nq, 0))),
        ],
        out_shape=[jax.ShapeDtypeStruct((m, HR), BF16), jax.ShapeDtypeStruct((m, n - HR), F32)],
        scratch_shapes=[pltpu.VMEM((tm, d), BF16)],
        compiler_params=_params(2),
        name="inproj",
    )(h, gain, w, q_gain)


DSA_TQ = 256
DSA_KC = 256
DSA_RB = 64


def _dsa_kernel(q_ref, iq_ref, ckv_ref, ik_ref, iw_ref, bias_ref, kn_ref, o_ref,
                ckv_s, ik_s, q_s, iq_s, wb_s, sc_s, sct_s, lg_s, m_s, l_s, acc_s, *, topk):
    qi = pl.program_id(1)
    tq, kc, nh, sub, rb = DSA_TQ, DSA_KC, ATT_HEADS, Q_BLOCK, DSA_RB
    nrb = tq // rb
    nchunk = qi + 1

    @pl.when(qi == 0)
    def _():
        ckv_s[...] = _rms(ckv_ref[...], kn_ref[...]).astype(BF16)
        ik_s[...] = ik_ref[:, :IDX_DIM].astype(BF16)

    for h in range(nh):
        q_s[h * tq:(h + 1) * tq, :] = q_ref[:, h * KV_RANK:(h + 1) * KV_RANK]
    w = iw_ref[:, IDX_DIM:IDX_DIM + IDX_HEADS] * (IQW ** -0.5)
    for r in range(nrb):
        for h in range(IDX_HEADS):
            iq_s[r, h * rb:(h + 1) * rb, :] = iq_ref[r * rb:(r + 1) * rb, h * IDX_DIM:(h + 1) * IDX_DIM].astype(BF16)
            wb_s[r, h] = jnp.broadcast_to(w[r * rb:(r + 1) * rb, h:h + 1], (rb, LANE))

    nt_dims = (((1,), (1,)), ((), ()))
    row_rb = lax.broadcasted_iota(jnp.int32, (rb, kc), 0)
    col_rb = lax.broadcasted_iota(jnp.int32, (rb, kc), 1)

    last = nchunk - 1

    def for_chunk_pairs(body, start):
        def pair(p, carry):
            c0 = start + 2 * p
            body(c0)
            body(jnp.minimum(c0 + 1, last))
            return carry

        lax.fori_loop(0, (nchunk - start + 1) // 2, pair, 0)

    def score_chunk(c):
        off = pl.multiple_of(c * kc, kc)
        ikc = ik_s[pl.ds(off, kc), :]
        for r in range(nrb):
            d = lax.dot_general(iq_s[r], ikc, nt_dims, preferred_element_type=F32)
            d = jnp.maximum(d, 0.0).reshape(IDX_HEADS, rb, kc)
            wb = wb_s[r]
            s = jnp.sum(d * jnp.concatenate([wb, wb], axis=-1), axis=0)
            causal = (c * kc + col_rb) <= (qi * tq + r * rb + row_rb)
            sc_s[c, r * rb:(r + 1) * rb, :] = jnp.where(causal, s, -jnp.inf)
        sct_s[c] = sc_s[c].T

    for_chunk_pairs(score_chunk, 0)

    npair = (nchunk + 1) // 2

    @pl.when(nchunk % 2 == 1)
    def _():
        sct_s[nchunk] = jnp.full((kc, tq), -jnp.inf, F32)

    kf = float(topk)

    def count(pred):
        def pair_body(p, acc):
            for c in (2 * p, 2 * p + 1):
                hit = jnp.where(pred(sct_s[c].reshape(kc // 32, 4, SUBLANE, tq)), 1.0, 0.0)
                acc = acc + jnp.sum(jnp.sum(hit, axis=1), axis=0)
            return acc

        acc = lax.fori_loop(0, npair, pair_body, jnp.zeros((SUBLANE, tq), F32))
        return jnp.sum(acc, axis=0, keepdims=True)

    def key_to_float(key):
        return lax.bitcast_convert_type(jnp.where(key >= 0, key, key ^ jnp.int32(0x7FFFFFFF)), F32)

    def bit_body(it, t_key):
        cand = t_key + lax.shift_left(jnp.int32(1), 31 - it)
        cf = key_to_float(cand)
        cnt = count(lambda st: st >= cf[None, None])
        ok = (cnt >= kf) | (cand < KEY_NEG_INF)
        return jnp.where(ok, cand, t_key)

    t_key = lax.cond(nchunk * kc > topk,
                     lambda: lax.fori_loop(0, 32, bit_body, jnp.full((SUBLANE, tq), -2 ** 31, jnp.int32)),
                     lambda: jnp.full((SUBLANE, tq), KEY_NEG_INF, jnp.int32))
    thr_row = jnp.maximum(key_to_float(t_key), F32_LOWEST)

    def to_rows(v):
        return jnp.broadcast_to(v[0:1, :], (kc, tq)).T

    n_gt = count(lambda st: st > thr_row[None, None])
    n_ge = count(lambda st: st >= thr_row[None, None])
    need = kf - n_gt

    @pl.when(jnp.max(n_ge - kf) > 0.0)
    def _():
        kidx_t = lax.broadcasted_iota(jnp.int32, (kc, tq), 0).astype(F32)

        def mark_body(c, carry):
            sct_s[c] = jnp.where(sct_s[c] == thr_row[0:1, :], kidx_t + (c * kc).astype(F32), jnp.inf)
            return carry

        lax.fori_loop(0, 2 * npair, mark_body, 0)

        def ybit_body(it, y):
            cand = y + lax.shift_left(jnp.int32(1), 11 - it)
            cnt = count(lambda tk: tk < cand.astype(F32)[None, None])
            return jnp.where(cnt < need, cand, y)

        y = lax.fori_loop(0, 12, ybit_body, jnp.zeros((SUBLANE, tq), jnp.int32))
        y_rows = to_rows(y.astype(F32))
        thr_rows = to_rows(thr_row)
        kcol = lax.broadcasted_iota(jnp.int32, (tq, kc), 1).astype(F32)

        def drop_body(c, carry):
            s = sc_s[c]
            drop = (s == thr_rows) & ((kcol + (c * kc).astype(F32)) > y_rows)
            sc_s[c] = jnp.where(drop, -jnp.inf, s)
            return carry

        lax.fori_loop(0, nchunk, drop_body, 0)

    thr = to_rows(thr_row)


    def bias_tile(c, a, b):
        return bias_ref[jnp.clip(2 * qi + a - 2 * c - b, 0, 2)]

    def logit_chunk(c, first=False):
        off = pl.multiple_of(c * kc, kc)
        lg = lax.dot_general(q_s[...], ckv_s[pl.ds(off, kc), :], nt_dims, preferred_element_type=F32)
        bias = jnp.concatenate(
            [jnp.concatenate([bias_tile(c, a, 0), bias_tile(c, a, 1)], axis=-1) for a in range(tq // sub)],
            axis=-2)
        keep = sc_s[c] >= thr
        lg = jnp.where(keep[None], lg.reshape(nh, tq, kc) + bias, -jnp.inf)
        lg_s[c] = lg
        mx = jnp.maximum(lg[..., :LANE], lg[..., LANE:])
        m_s[...] = mx if first else jnp.maximum(m_s[...], mx)

    logit_chunk(0, first=True)
    for_chunk_pairs(logit_chunk, 1)

    m = jnp.max(m_s[...], axis=-1, keepdims=True)
    m_s[...] = jnp.broadcast_to(m, m_s.shape)

    def pv_chunk(c, first=False):
        off = pl.multiple_of(c * kc, kc)
        lg = lg_s[c]
        mb = m_s[...]
        p0 = jnp.exp2(lg[..., :LANE] - mb)
        p1 = jnp.exp2(lg[..., LANE:] - mb)
        p = jnp.concatenate([p0, p1], axis=-1).reshape(nh * tq, kc).astype(BF16)
        pv = jnp.dot(p, ckv_s[pl.ds(off, kc), :], preferred_element_type=F32)
        if first:
            l_s[...] = p0 + p1
            acc_s[...] = pv
        else:
            l_s[...] += p0 + p1
            acc_s[...] += pv

    pv_chunk(0, first=True)

    def pv_pair(p, carry):
        pv_chunk(1 + 2 * p)
        pv_chunk(2 + 2 * p)
        return carry

    lax.fori_loop(0, (nchunk - 1) // 2, pv_pair, 0)

    @pl.when(nchunk % 2 == 0)
    def _():
        pv_chunk(last)

    lsum = jnp.sum(l_s[...], axis=-1, keepdims=True)
    for h in range(nh):
        o_ref[:, h * KV_RANK:(h + 1) * KV_RANK] = (acc_s[h * tq:(h + 1) * tq, :] / lsum[h]).astype(o_ref.dtype)


def _dsa(q3, p3, bias_tiles, kv_gain, topk):
    b, s, _ = p3.shape
    tq, kc, sub, rb = DSA_TQ, DSA_KC, Q_BLOCK, DSA_RB
    assert tq == kc and s % tq == 0 and s <= 2 ** 11
    nkc = s // kc
    assert nkc % 2 == 0
    kern = functools.partial(_dsa_kernel, topk=topk)
    return pl.pallas_call(
        kern,
        grid=(b, s // tq),
        in_specs=[
            pl.BlockSpec((None, tq, HR), lambda i, j: (i, j, 0)),
            pl.BlockSpec((None, tq, IQW), lambda i, j: (i, j, OFF_IQ // IQW)),
            pl.BlockSpec((None, s, KV_RANK), lambda i, j: (i, 0, OFF_CKV // KV_RANK)),
            pl.BlockSpec((None, s, LANE), lambda i, j: (i, 0, OFF_IKW // LANE)),
            pl.BlockSpec((None, tq, LANE), lambda i, j: (i, j, OFF_IKW // LANE)),
            pl.BlockSpec((3, ATT_HEADS, sub, sub), lambda i, j: (0, 0, 0, 0)),
            pl.BlockSpec((1, KV_RANK), lambda i, j: (0, 0)),
        ],
        out_specs=pl.BlockSpec((None, tq, HR), lambda i, j: (i, j, 0)),
        out_shape=jax.ShapeDtypeStruct((b, s, HR), BF16),
        scratch_shapes=[
            pltpu.VMEM((s, KV_RANK), BF16),
            pltpu.VMEM((s, IDX_DIM), BF16),
            pltpu.VMEM((ATT_HEADS * tq, KV_RANK), BF16),
            pltpu.VMEM((tq // rb, IDX_HEADS * rb, IDX_DIM), BF16),
            pltpu.VMEM((tq // rb, IDX_HEADS, rb, LANE), F32),
            pltpu.VMEM((nkc, tq, kc), F32),
            pltpu.VMEM((nkc, kc, tq), F32),
            pltpu.VMEM((nkc, ATT_HEADS, tq, kc), F32),
            pltpu.VMEM((ATT_HEADS, tq, LANE), F32),
            pltpu.VMEM((ATT_HEADS, tq, LANE), F32),
            pltpu.VMEM((ATT_HEADS * tq, KV_RANK), F32),
        ],
        compiler_params=_params(2),
        name="dsa",
    )(q3, p3, p3, p3, p3, bias_tiles, kv_gain)


MLSTM_CHUNK = 128


def _mlstm_kernel(mqk_ref, mv_ref, mo_ref, gt_ref, cw_ref, cb_ref, gb_ref, on_ref, o_ref,
                  c_s, n_s, m_s, ext_s):
    lc = MLSTM_CHUNK
    dk, dv = M_QK_DIM, M_V_DIM

    @pl.when(pl.program_id(1) == 0)
    def _():
        c_s[...] = jnp.zeros(c_s.shape, F32)
        n_s[...] = jnp.zeros(n_s.shape, F32)
        m_s[...] = jnp.zeros(m_s.shape, F32)
        ext_s[0:SUBLANE, :] = jnp.zeros((SUBLANE, MQK), F32)

    x = mqk_ref[...]
    ext_s[SUBLANE:SUBLANE + lc, :] = x
    conv = jnp.broadcast_to(cb_ref[...], (lc, MQK))
    for j in range(CONV_WIDTH):
        lo = SUBLANE - (CONV_WIDTH - 1) + j
        conv = conv + cw_ref[j:j + 1, :] * ext_s[lo:lo + lc, :]
    ext_s[0:SUBLANE, :] = x[lc - SUBLANE:lc, :]
    a = conv * _sigmoid(conv)

    gates = gt_ref[...] + gb_ref[...]
    lsg = _log_sigmoid(gates)
    r_i = lax.broadcasted_iota(jnp.int32, (lc, lc), 0)
    c_i = lax.broadcasted_iota(jnp.int32, (lc, lc), 1)
    tril = r_i >= c_i
    hi = lax.Precision.HIGHEST
    bcum_col = jnp.dot(tril.astype(F32), lsg, precision=hi, preferred_element_type=F32)
    gates_t = gates.T
    bcum_row = jnp.dot(lsg.T, (r_i <= c_i).astype(F32), precision=hi, preferred_element_type=F32)

    for h in range(M_HEADS):
        bc = bcum_col[:, M_HEADS + h:M_HEADS + h + 1]
        br = bcum_row[M_HEADS + h:M_HEADS + h + 1, :]
        ic = gates[:, h:h + 1]
        ir = gates_t[h:h + 1, :]
        mst = m_s[h, 0:1, 0:1]
        q = a[:, h * dk:(h + 1) * dk]
        k = a[:, MQK // 2 + h * dk:MQK // 2 + (h + 1) * dk] * (dk ** -0.5)
        v = mv_ref[:, h * dv:(h + 1) * dv]
        qb16, kb16, vb16 = q.astype(BF16), k.astype(BF16), v.astype(BF16)

        dmat = jnp.where(tril, bc - br + ir, -jnp.inf)
        inter = bc + mst
        m_row = jnp.maximum(inter, jnp.max(dmat, axis=-1, keepdims=True))
        w_inter = jnp.exp(inter - m_row)
        qk = lax.dot_general(qb16, kb16, (((1,), (1,)), ((), ())), preferred_element_type=F32)
        qk = qk * jnp.exp(dmat - m_row)
        num = (w_inter * jnp.dot(qb16, c_s[h].astype(BF16), preferred_element_type=F32)
               + jnp.dot(qk.astype(BF16), vb16, preferred_element_type=F32))
        den = (w_inter * jnp.sum(q * n_s[h, 0:1, :], axis=-1, keepdims=True)
               + jnp.sum(qk, axis=-1, keepdims=True))
        hh = num / jnp.maximum(jnp.abs(den), jnp.exp(-m_row))

        b_last = bc[lc - 1:lc, :]
        g_col = b_last - bc + ic
        g_row = b_last - br + ir
        m_new = jnp.maximum(b_last + mst, jnp.max(g_row, axis=-1, keepdims=True))
        decay = jnp.exp(b_last + mst - m_new)
        kw = jnp.exp(g_col - m_new) * k
        c_s[h] = decay * c_s[h] + lax.dot_general(kw.astype(BF16), vb16, (((0,), (0,)), ((), ())),
                                                   preferred_element_type=F32)
        n_s[h, 0:1, :] = decay * n_s[h, 0:1, :] + jnp.sum(kw, axis=0, keepdims=True)
        m_s[h] = jnp.broadcast_to(m_new, (SUBLANE, LANE))

        hn = _rms(hh, on_ref[:, h * dv:(h + 1) * dv])
        o_ref[:, h * dv:(h + 1) * dv] = (hn * _sigmoid(mo_ref[:, h * dv:(h + 1) * dv])).astype(o_ref.dtype)


def _mlstm(p3, conv_w, conv_b, gate_b, out_gain):
    b, s, _ = p3.shape
    lc = MLSTM_CHUNK
    assert s % lc == 0 and lc == LANE
    return pl.pallas_call(
        _mlstm_kernel,
        grid=(b, s // lc),
        in_specs=[
            pl.BlockSpec((None, lc, MQK), lambda i, j: (i, j, OFF_MQK // MQK)),
            pl.BlockSpec((None, lc, MV), lambda i, j: (i, j, OFF_MV // MV)),
            pl.BlockSpec((None, lc, MV), lambda i, j: (i, j, OFF_MO // MV)),
            pl.BlockSpec((None, lc, LANE), lambda i, j: (i, j, OFF_MG // LANE)),
            pl.BlockSpec((CONV_WIDTH, MQK), lambda i, j: (0, 0)),
            pl.BlockSpec((1, MQK), lambda i, j: (0, 0)),
            pl.BlockSpec((1, LANE), lambda i, j: (0, 0)),
            pl.BlockSpec((1, MV), lambda i, j: (0, 0)),
        ],
        out_specs=pl.BlockSpec((None, lc, MV), lambda i, j: (i, j, 0)),
        out_shape=jax.ShapeDtypeStruct((b, s, MV), BF16),
        scratch_shapes=[
            pltpu.VMEM((M_HEADS, M_QK_DIM, M_V_DIM), F32),
            pltpu.VMEM((M_HEADS, SUBLANE, M_QK_DIM), F32),
            pltpu.VMEM((M_HEADS, SUBLANE, LANE), F32),
            pltpu.VMEM((SUBLANE + lc, MQK), F32),
        ],
        compiler_params=_params(2),
        name="mlstm",
    )(p3, p3, p3, p3, conv_w, conv_b, gate_b, out_gain)


def _mergeout_kernel(oa_ref, hm_ref, ga_ref, gm_ref, h_ref, wa_ref, wm_ref, wo_ref, o_ref, mg_s, *, tn):
    d = o_ref.shape[1]
    oa = oa_ref[...]
    hm = hm_ref[...]
    for n in range(d // tn):
        sl = slice(n * tn, (n + 1) * tn)
        ya = jnp.dot(oa, wa_ref[:, sl], preferred_element_type=F32)
        ym = jnp.dot(hm, wm_ref[:, sl], preferred_element_type=F32)
        mg_s[:, sl] = (_sigmoid(ga_ref[:, sl]) * ya + _sigmoid(gm_ref[:, sl]) * ym).astype(BF16)
    mg = mg_s[...]
    for n in range(d // tn):
        sl = slice(n * tn, (n + 1) * tn)
        o_ref[:, sl] = h_ref[:, sl] + jnp.dot(mg, wo_ref[:, sl], preferred_element_type=F32)


def _mergeout(oa, hm, p2, h, wa, wm, wo, l):
    m, d = h.shape
    tm = _tile(m, 256)
    tn = _tile(d, 512)
    assert OFF_GA % d == 0
    ga0 = OFF_GA // d
    resident = functools.partial(pl.BlockSpec, pipeline_mode=pl.Buffered(1))
    return pl.pallas_call(
        functools.partial(_mergeout_kernel, tn=tn),
        grid=(m // tm,),
        in_specs=[
            pl.BlockSpec((tm, HR), lambda i: (i, 0)),
            pl.BlockSpec((tm, MV), lambda i: (i, 0)),
            pl.BlockSpec((tm, d), lambda i: (i, ga0)),
            pl.BlockSpec((tm, d), lambda i: (i, ga0 + 1)),
            pl.BlockSpec((tm, d), lambda i: (i, 0)),
            resident((None, HR, d), lambda i: (l, 0, 0)),
            resident((None, MV, d), lambda i: (l, 0, 0)),
            resident((None, d, d), lambda i: (l, 0, 0)),
        ],
        out_specs=pl.BlockSpec((tm, d), lambda i: (i, 0)),
        out_shape=jax.ShapeDtypeStruct((m, d), F32),
        scratch_shapes=[pltpu.VMEM((tm, d), BF16)],
        compiler_params=_params(1),
        name="mergeout",
    )(oa, hm, p2, p2, h, wa, wm, wo)


def _bias_tiles(rel_bias):
    tq = Q_BLOCK
    i = jnp.arange(tq)[:, None]
    j = jnp.arange(tq)[None, :]

    def bucket(rel):
        n = jnp.maximum(rel, 0)
        max_exact = REL_BUCKETS // 2
        n_large = jnp.maximum(n, max_exact).astype(F32)
        large = max_exact + (jnp.log(n_large / max_exact) / math.log(REL_MAX_DIST / max_exact)
                             * (REL_BUCKETS - max_exact)).astype(jnp.int32)
        large = jnp.minimum(large, REL_BUCKETS - 1)
        return jnp.where(n < max_exact, n, large)

    onehot = jax.nn.one_hot(jnp.stack([bucket(i - j + d * tq) for d in range(3)]), REL_BUCKETS, dtype=F32)
    return jnp.einsum("dijk,kh->dhij", onehot, rel_bias.astype(F32) * LOG2E, precision=lax.Precision.HIGHEST)


def _w_in_kernel(w_ref, o_ref, *, segs):
    for dst, src, width in segs:
        if src is None:
            o_ref[0, dst:dst + width, :] = jnp.zeros((width, o_ref.shape[2]), o_ref.dtype)
        else:
            o_ref[0, dst:dst + width, :] = w_ref[0, src:src + width, :].astype(o_ref.dtype)


def _prep_w_in(w_in, d):
    depth, dm, din = w_in.shape
    widths = (HR, KV_RANK, IQW, IDX_DIM, IDX_HEADS, MQK, MV, MV, M_HEADS, M_HEADS, d, d)
    offs = [0]
    for wdt in widths:
        offs.append(offs[-1] + wdt)
    assert offs[-1] == din
    aq, ckv, iq, ik, _, mqk, _, _, mi, _, ga, _ = offs[:12]
    pieces = [(aq, HR), (iq, IQW), (ckv, KV_RANK), (ik, IDX_DIM + IDX_HEADS), (None, LANE - IDX_DIM - IDX_HEADS),
              (mi, 2 * M_HEADS), (None, LANE - 2 * M_HEADS), (mqk, MQK + 2 * MV), (ga, 2 * d)]
    segs = []
    dst = 0
    for src, width in pieces:
        segs.append((dst, src, width))
        dst += width
    assert all(o % SUBLANE == 0 and (src is None or src % SUBLANE == 0) for o, src, _ in segs)
    tc = _tile(dm, 256)
    return pl.pallas_call(
        functools.partial(_w_in_kernel, segs=tuple(segs)),
        grid=(depth, dm // tc),
        in_specs=[pl.BlockSpec((1, din, tc), lambda l, i: (l, 0, i))],
        out_specs=pl.BlockSpec((1, dst, tc), lambda l, i: (l, 0, i)),
        out_shape=jax.ShapeDtypeStruct((depth, dst, dm), BF16),
        compiler_params=_params(2),
        name="prep_w_in",
    )(jnp.swapaxes(w_in, 1, 2))


def kernel(x, rel_bias, ffn1_norm, ffn1_w_gu, ffn1_w_down, mix_norm, w_in, q_norm, kv_norm, conv_w, conv_b,
           igate_b, fgate_b, m_out_norm, w_att_out, w_mem_out, w_out, ffn2_norm, ffn2_w_gu, ffn2_w_down):
    b, s, d = x.shape
    depth = w_in.shape[0]
    topk = min(TOPK_MAX, s // 4)
    m = b * s
    h = x.reshape(m, d)
    bias_tiles = _bias_tiles(rel_bias)
    w1gu, w2gu = ffn1_w_gu.astype(BF16), ffn2_w_gu.astype(BF16)
    w1d, w2d = ffn1_w_down, ffn2_w_down
    w_in_r = _prep_w_in(w_in, d)
    w_att = w_att_out.astype(BF16).reshape(depth, HR, d)
    w_mem = w_mem_out.astype(BF16)
    w_o = w_out.astype(BF16)
    gate_b = jnp.concatenate([igate_b, fgate_b, jnp.zeros((depth, LANE - 2 * M_HEADS), F32)], axis=1)
    for l in range(depth):
        h = _ffn(h, ffn1_norm[l][None], w1gu, w1d, l)

        q2, p2 = _inproj(h, mix_norm[l][None], w_in_r, q_norm[l][None], l)
        p3 = p2.reshape(b, s, p2.shape[1])
        o_att = _dsa(q2.reshape(b, s, HR), p3, bias_tiles, kv_norm[l][None], topk)
        hm = _mlstm(p3, conv_w[l], conv_b[l][None], gate_b[l][None], m_out_norm[l].reshape(1, MV))
        h = _mergeout(o_att.reshape(m, HR), hm.reshape(m, MV), p2, h, w_att, w_mem, w_o, l)

        h = _ffn(h, ffn2_norm[l][None], w2gu, w2d, l)
    return h.reshape(b, s, d)
```

```python
import functools
import math

import jax
import jax.numpy as jnp
from jax import lax
from jax.experimental import pallas as pl
from jax.experimental.pallas import tpu as pltpu

F32 = jnp.float32
BF16 = jnp.bfloat16

ATT_HEADS = 8
KV_RANK = 256
IDX_HEADS = 8
IDX_DIM = 64
TOPK_MAX = 256
Q_BLOCK = 128
REL_BUCKETS = 32
REL_MAX_DIST = 128
M_HEADS = 4
M_QK_DIM = 128
M_V_DIM = 256
CONV_WIDTH = 4
EPS = 1e-6

HR = ATT_HEADS * KV_RANK
IQW = IDX_HEADS * IDX_DIM
MQK = 2 * M_HEADS * M_QK_DIM
MV = M_HEADS * M_V_DIM
LANE = 128
SUBLANE = 8
VMEM_LIMIT = 56 * 1024 * 1024
FFN_VMEM_LIMIT = 60 * 1024 * 1024

OFF_IQ = 0
OFF_CKV = OFF_IQ + IQW
OFF_IKW = OFF_CKV + KV_RANK
OFF_MG = OFF_IKW + LANE
OFF_MQK = OFF_MG + LANE
OFF_MV = OFF_MQK + MQK
OFF_MO = OFF_MV + MV
OFF_GA = OFF_MO + MV

KEY_NEG_INF = -2139095041
LOG2E = math.log2(math.e)
Q_SCALE = KV_RANK ** -0.5 * LOG2E
F32_LOWEST = float(jnp.finfo(jnp.float32).min)


def _params(n_grid, vmem_limit=VMEM_LIMIT):
    return pltpu.CompilerParams(dimension_semantics=("arbitrary",) * n_grid,
                                vmem_limit_bytes=vmem_limit)


def _tile(n, pref):
    if n <= pref:
        return n
    t = (pref // LANE) * LANE
    while n % t:
        t -= LANE
    return t


def _rms(x, g):
    return x * lax.rsqrt(jnp.mean(x * x, axis=-1, keepdims=True) + EPS) * g


def _sigmoid(x):
    return 1.0 / (1.0 + jnp.exp(-x))


def _log_sigmoid(x):
    return jnp.minimum(x, 0.0) - jnp.log(1.0 + jnp.exp(-jnp.abs(x)))


FFN_NORM_CHUNKS = 4


def _ffn_kernel(h_ref, g_ref, wg_ref, wu_ref, wd_ref, o_ref, xn_ref, *, overlap):
    j = pl.program_id(1)

    def step(first):
        xn = xn_ref[...]
        g = jnp.dot(xn, wg_ref[0], preferred_element_type=F32)
        u = jnp.dot(xn, wu_ref[0], preferred_element_type=F32)
        a = g * _sigmoid(g) * u * 0.5
        if overlap:
            col = lax.broadcasted_iota(jnp.int32, a.shape, 1)
            a = jnp.where((j == pl.num_programs(1) - 1) & (col < overlap), 0.0, a)
        y = jnp.dot(a.astype(BF16), wd_ref[0].astype(BF16), preferred_element_type=F32)
        if first:
            o_ref[...] = h_ref[...] + y
        else:
            o_ref[...] += y

    @pl.when(j == 0)
    def _():
        rows = h_ref.shape[0] // FFN_NORM_CHUNKS
        for r in range(FFN_NORM_CHUNKS):
            sl = slice(r * rows, (r + 1) * rows)
            xn_ref[sl, :] = _rms(h_ref[sl, :], g_ref[...]).astype(BF16)
        step(True)

    @pl.when(j > 0)
    def _():
        step(False)


def _ffn(h, gain, w_gu, w_down, l):
    m, d = h.shape
    f = w_down.shape[1]
    tm = _tile(m, 1024)
    tf = min(f, 512)
    nj = -(-f // tf)
    assert f % LANE == 0 and tf % LANE == 0

    def off(j):
        return pl.multiple_of(jnp.minimum(j * tf, f - tf), LANE)

    return pl.pallas_call(
        functools.partial(_ffn_kernel, overlap=nj * tf - f),
        grid=(m // tm, nj),
        in_specs=[
            pl.BlockSpec((tm, d), lambda i, j: (i, 0)),
            pl.BlockSpec((1, d), lambda i, j: (0, 0)),
            pl.BlockSpec((pl.Element(1), pl.Element(d), pl.Element(tf)), lambda i, j: (l, 0, off(j))),
            pl.BlockSpec((pl.Element(1), pl.Element(d), pl.Element(tf)),
                         lambda i, j: (l, 0, pl.multiple_of(f + off(j), LANE))),
            pl.BlockSpec((pl.Element(1), pl.Element(tf), pl.Element(d)), lambda i, j: (l, off(j), 0)),
        ],
        out_specs=pl.BlockSpec((tm, d), lambda i, j: (i, 0)),
        out_shape=jax.ShapeDtypeStruct((m, d), F32),
        scratch_shapes=[pltpu.VMEM((tm, d), BF16)],
        compiler_params=_params(2, FFN_VMEM_LIMIT),
        name="ffn",
    )(h, gain, w_gu, w_gu, w_down)


def _inproj_kernel(h_ref, g_ref, w_ref, qn_ref, q_ref, p_ref, xn_ref, *, nq):
    j = pl.program_id(1)
    nt_dims = (((1,), (1,)), ((), ()))

    def q_step():
        acc = lax.dot_general(xn_ref[...], w_ref[...], nt_dims, preferred_element_type=F32)
        for k in range(acc.shape[1] // KV_RANK):
            sl = slice(k * KV_RANK, (k + 1) * KV_RANK)
            q_ref[:, sl] = (_rms(acc[:, sl], qn_ref[...]) * Q_SCALE).astype(q_ref.dtype)

    @pl.when(j == 0)
    def _():
        rows = h_ref.shape[0] // FFN_NORM_CHUNKS
        for r in range(FFN_NORM_CHUNKS):
            sl = slice(r * rows, (r + 1) * rows)
            xn_ref[sl, :] = _rms(h_ref[sl, :], g_ref[...]).astype(BF16)
        q_step()

    @pl.when((j > 0) & (j < nq))
    def _():
        q_step()

    @pl.when(j >= nq)
    def _():
        p_ref[...] = lax.dot_general(xn_ref[...], w_ref[...], nt_dims, preferred_element_type=F32)


def _inproj(h, gain, w, q_gain, l):
    m, d = h.shape
    n = w.shape[1]
    tm = _tile(m, 1024)
    tn = 1024 if n % 1024 == 0 else 512
    assert HR % tn == 0 and n % tn == 0 and tn % KV_RANK == 0
    nq = HR // tn
    return pl.pallas_call(
        functools.partial(_inproj_kernel, nq=nq),
        grid=(m // tm, n // tn),
        in_specs=[
            pl.BlockSpec((tm, d), lambda i, j: (i, 0)),
            pl.BlockSpec((1, d), lambda i, j: (0, 0)),
            pl.BlockSpec((None, tn, d), lambda i, j: (l, j, 0)),
            pl.BlockSpec((1, KV_RANK), lambda i, j: (0, 0)),
        ],
        out_specs=[
            pl.BlockSpec((tm, tn), lambda i, j: (i, jnp.minimum(j, nq - 1))),
            pl.BlockSpec((tm, tn), lambda i, j: (i, jnp.maximum(j - nq, 0))),
        ],
        out_shape=[jax.ShapeDtypeStruct((m, HR), BF16), jax.ShapeDtypeStruct((m, n - HR), F32)],
        scratch_shapes=[pltpu.VMEM((tm, d), BF16)],
        compiler_params=_params(2),
        name="inproj",
    )(h, gain, w, q_gain)


DSA_TQ = 256
DSA_KC = 256
DSA_RB = 64


def _dsa_kernel(q_ref, iq_ref, ckv_ref, ik_ref, iw_ref, bias_ref, kn_ref, o_ref,
                ckv_s, ik_s, q_s, iq_s, wb_s, sc_s, sct_s, lg_s, m_s, l_s, acc_s, *, topk):
    qi = pl.program_id(1)
    tq, kc, nh, sub, rb = DSA_TQ, DSA_KC, ATT_HEADS, Q_BLOCK, DSA_RB
    nrb = tq // rb
    nchunk = qi + 1

    @pl.when(qi == 0)
    def _():
        ckv_s[...] = _rms(ckv_ref[...], kn_ref[...]).astype(BF16)
        ik_s[...] = ik_ref[:, :IDX_DIM].astype(BF16)

    for h in range(nh):
        q_s[h * tq:(h + 1) * tq, :] = q_ref[:, h * KV_RANK:(h + 1) * KV_RANK]
    w = iw_ref[:, IDX_DIM:IDX_DIM + IDX_HEADS] * (IQW ** -0.5)
    for r in range(nrb):
        for h in range(IDX_HEADS):
            iq_s[r, h * rb:(h + 1) * rb, :] = iq_ref[r * rb:(r + 1) * rb, h * IDX_DIM:(h + 1) * IDX_DIM].astype(BF16)
            wb_s[r, h] = jnp.broadcast_to(w[r * rb:(r + 1) * rb, h:h + 1], (rb, LANE))

    nt_dims = (((1,), (1,)), ((), ()))
    row_rb = lax.broadcasted_iota(jnp.int32, (rb, kc), 0)
    col_rb = lax.broadcasted_iota(jnp.int32, (rb, kc), 1)

    last = nchunk - 1

    def for_chunk_pairs(body, start):
        def pair(p, carry):
            c0 = start + 2 * p
            body(c0)
            body(jnp.minimum(c0 + 1, last))
            return carry

        lax.fori_loop(0, (nchunk - start + 1) // 2, pair, 0)

    def score_chunk(c):
        off = pl.multiple_of(c * kc, kc)
        ikc = ik_s[pl.ds(off, kc), :]
        for r in range(nrb):
            d = lax.dot_general(iq_s[r], ikc, nt_dims, preferred_element_type=F32)
            d = jnp.maximum(d, 0.0).reshape(IDX_HEADS, rb, kc)
            wb = wb_s[r]
            s = jnp.sum(d * jnp.concatenate([wb, wb], axis=-1), axis=0)
            causal = (c * kc + col_rb) <= (qi * tq + r * rb + row_rb)
            sc_s[c, r * rb:(r + 1) * rb, :] = jnp.where(causal, s, -jnp.inf)
        sct_s[c] = sc_s[c].T

    for_chunk_pairs(score_chunk, 0)

    npair = (nchunk + 1) // 2

    @pl.when(nchunk % 2 == 1)
    def _():
        sct_s[nchunk] = jnp.full((kc, tq), -jnp.inf, F32)

    kf = float(topk)

    def count(pred):
        def pair_body(p, acc):
            for c in (2 * p, 2 * p + 1):
                hit = jnp.where(pred(sct_s[c].reshape(kc // 32, 4, SUBLANE, tq)), 1.0, 0.0)
                acc = acc + jnp.sum(jnp.sum(hit, axis=1), axis=0)
            return acc

        acc = lax.fori_loop(0, npair, pair_body, jnp.zeros((SUBLANE, tq), F32))
        return jnp.sum(acc, axis=0, keepdims=True)

    def key_to_float(key):
        return lax.bitcast_convert_type(jnp.where(key >= 0, key, key ^ jnp.int32(0x7FFFFFFF)), F32)

    def bit_body(it, t_key):
        cand = t_key + lax.shift_left(jnp.int32(1), 31 - it)
        cf = key_to_float(cand)
        cnt = count(lambda st: st >= cf[None, None])
        ok = (cnt >= kf) | (cand < KEY_NEG_INF)
        return jnp.where(ok, cand, t_key)

    t_key = lax.cond(nchunk * kc > topk,
                     lambda: lax.fori_loop(0, 32, bit_body, jnp.full((SUBLANE, tq), -2 ** 31, jnp.int32)),
                     lambda: jnp.full((SUBLANE, tq), KEY_NEG_INF, jnp.int32))
    thr_row = jnp.maximum(key_to_float(t_key), F32_LOWEST)

    def to_rows(v):
        return jnp.broadcast_to(v[0:1, :], (kc, tq)).T

    n_gt = count(lambda st: st > thr_row[None, None])
    n_ge = count(lambda st: st >= thr_row[None, None])
    need = kf - n_gt

    @pl.when(jnp.max(n_ge - kf) > 0.0)
    def _():
        kidx_t = lax.broadcasted_iota(jnp.int32, (kc, tq), 0).astype(F32)

        def mark_body(c, carry):
            sct_s[c] = jnp.where(sct_s[c] == thr_row[0:1, :], kidx_t + (c * kc).astype(F32), jnp.inf)
            return carry

        lax.fori_loop(0, 2 * npair, mark_body, 0)

        def ybit_body(it, y):
            cand = y + lax.shift_left(jnp.int32(1), 11 - it)
            cnt = count(lambda tk: tk < cand.astype(F32)[None, None])
            return jnp.where(cnt < need, cand, y)

        y = lax.fori_loop(0, 12, ybit_body, jnp.zeros((SUBLANE, tq), jnp.int32))
        y_rows = to_rows(y.astype(F32))
        thr_rows = to_rows(thr_row)
        kcol = lax.broadcasted_iota(jnp.int32, (tq, kc), 1).astype(F32)

        def drop_body(c, carry):
            s = sc_s[c]
            drop = (s == thr_rows) & ((kcol + (c * kc).astype(F32)) > y_rows)
            sc_s[c] = jnp.where(drop, -jnp.inf, s)
            return carry

        lax.fori_loop(0, nchunk, drop_body, 0)

    thr = to_rows(thr_row)


    def bias_tile(c, a, b):
        return bias_ref[jnp.clip(2 * qi + a - 2 * c - b, 0, 2)]

    def logit_chunk(c, first=False):
        off = pl.multiple_of(c * kc, kc)
        lg = lax.dot_general(q_s[...], ckv_s[pl.ds(off, kc), :], nt_dims, preferred_element_type=F32)
        bias = jnp.concatenate(
            [jnp.concatenate([bias_tile(c, a, 0), bias_tile(c, a, 1)], axis=-1) for a in range(tq // sub)],
            axis=-2)
        keep = sc_s[c] >= thr
        lg = jnp.where(keep[None], lg.reshape(nh, tq, kc) + bias, -jnp.inf)
        lg_s[c] = lg
        mx = jnp.maximum(lg[..., :LANE], lg[..., LANE:])
        m_s[...] = mx if first else jnp.maximum(m_s[...], mx)

    logit_chunk(0, first=True)
    for_chunk_pairs(logit_chunk, 1)

    m = jnp.max(m_s[...], axis=-1, keepdims=True)
    m_s[...] = jnp.broadcast_to(m, m_s.shape)

    def pv_chunk(c, first=False):
        off = pl.multiple_of(c * kc, kc)
        lg = lg_s[c]
        mb = m_s[...]
        p0 = jnp.exp2(lg[..., :LANE] - mb)
        p1 = jnp.exp2(lg[..., LANE:] - mb)
        p = jnp.concatenate([p0, p1], axis=-1).reshape(nh * tq, kc).astype(BF16)
        pv = jnp.dot(p, ckv_s[pl.ds(off, kc), :], preferred_element_type=F32)
        if first:
            l_s[...] = p0 + p1
            acc_s[...] = pv
        else:
            l_s[...] += p0 + p1
            acc_s[...] += pv

    pv_chunk(0, first=True)

    def pv_pair(p, carry):
        pv_chunk(1 + 2 * p)
        pv_chunk(2 + 2 * p)
        return carry

    lax.fori_loop(0, (nchunk - 1) // 2, pv_pair, 0)

    @pl.when(nchunk % 2 == 0)
    def _():
        pv_chunk(last)

    lsum = jnp.sum(l_s[...], axis=-1, keepdims=True)
    for h in range(nh):
        o_ref[:, h * KV_RANK:(h + 1) * KV_RANK] = (acc_s[h * tq:(h + 1) * tq, :] / lsum[h]).astype(o_ref.dtype)


def _dsa(q3, p3, bias_tiles, kv_gain, topk):
    b, s, _ = p3.shape
    tq, kc, sub, rb = DSA_TQ, DSA_KC, Q_BLOCK, DSA_RB
    assert tq == kc and s % tq == 0 and s <= 2 ** 11
    nkc = s // kc
    assert nkc % 2 == 0
    kern = functools.partial(_dsa_kernel, topk=topk)
    return pl.pallas_call(
        kern,
        grid=(b, s // tq),
        in_specs=[
            pl.BlockSpec((None, tq, HR), lambda i, j: (i, j, 0)),
            pl.BlockSpec((None, tq, IQW), lambda i, j: (i, j, OFF_IQ // IQW)),
            pl.BlockSpec((None, s, KV_RANK), lambda i, j: (i, 0, OFF_CKV // KV_RANK)),
            pl.BlockSpec((None, s, LANE), lambda i, j: (i, 0, OFF_IKW // LANE)),
            pl.BlockSpec((None, tq, LANE), lambda i, j: (i, j, OFF_IKW // LANE)),
            pl.BlockSpec((3, ATT_HEADS, sub, sub), lambda i, j: (0, 0, 0, 0)),
            pl.BlockSpec((1, KV_RANK), lambda i, j: (0, 0)),
        ],
        out_specs=pl.BlockSpec((None, tq, HR), lambda i, j: (i, j, 0)),
        out_shape=jax.ShapeDtypeStruct((b, s, HR), BF16),
        scratch_shapes=[
            pltpu.VMEM((s, KV_RANK), BF16),
            pltpu.VMEM((s, IDX_DIM), BF16),
            pltpu.VMEM((ATT_HEADS * tq, KV_RANK), BF16),
            pltpu.VMEM((tq // rb, IDX_HEADS * rb, IDX_DIM), BF16),
            pltpu.VMEM((tq // rb, IDX_HEADS, rb, LANE), F32),
            pltpu.VMEM((nkc, tq, kc), F32),
            pltpu.VMEM((nkc, kc, tq), F32),
            pltpu.VMEM((nkc, ATT_HEADS, tq, kc), F32),
            pltpu.VMEM((ATT_HEADS, tq, LANE), F32),
            pltpu.VMEM((ATT_HEADS, tq, LANE), F32),
            pltpu.VMEM((ATT_HEADS * tq, KV_RANK), F32),
        ],
        compiler_params=_params(2),
        name="dsa",
    )(q3, p3, p3, p3, p3, bias_tiles, kv_gain)


MLSTM_CHUNK = 256


def _mlstm_kernel(mqk_ref, mv_ref, mo_ref, gt_ref, cw_ref, cb_ref, gb_ref, on_ref, o_ref,
                  c_s, n_s, m_s, ext_s):
    lc = MLSTM_CHUNK
    dk, dv = M_QK_DIM, M_V_DIM

    @pl.when(pl.program_id(1) == 0)
    def _():
        c_s[...] = jnp.zeros(c_s.shape, F32)
        n_s[...] = jnp.zeros(n_s.shape, F32)
        m_s[...] = jnp.zeros(m_s.shape, F32)
        ext_s[0:SUBLANE, :] = jnp.zeros((SUBLANE, MQK), F32)

    x = mqk_ref[...]
    ext_s[SUBLANE:SUBLANE + lc, :] = x
    conv = jnp.broadcast_to(cb_ref[...], (lc, MQK))
    for j in range(CONV_WIDTH):
        lo = SUBLANE - (CONV_WIDTH - 1) + j
        conv = conv + cw_ref[j:j + 1, :] * ext_s[lo:lo + lc, :]
    ext_s[0:SUBLANE, :] = x[lc - SUBLANE:lc, :]
    a = conv * _sigmoid(conv)

    gates = gt_ref[...] + gb_ref[...]
    lsg = _log_sigmoid(gates)
    r_i = lax.broadcasted_iota(jnp.int32, (lc, lc), 0)
    c_i = lax.broadcasted_iota(jnp.int32, (lc, lc), 1)
    tril = r_i >= c_i
    hi = lax.Precision.HIGHEST
    bcum_col = jnp.dot(tril.astype(F32), lsg, precision=hi, preferred_element_type=F32)
    gates_t = gates.T
    bcum_row = jnp.dot(lsg.T, (r_i <= c_i).astype(F32), precision=hi, preferred_element_type=F32)

    for h in range(M_HEADS):
        bc = bcum_col[:, M_HEADS + h:M_HEADS + h + 1]
        br = bcum_row[M_HEADS + h:M_HEADS + h + 1, :]
        ic = gates[:, h:h + 1]
        ir = gates_t[h:h + 1, :]
        mst = m_s[h, 0:1, 0:1]
        q = a[:, h * dk:(h + 1) * dk]
        k = a[:, MQK // 2 + h * dk:MQK // 2 + (h + 1) * dk] * (dk ** -0.5)
        v = mv_ref[:, h * dv:(h + 1) * dv]
        qb16, kb16, vb16 = q.astype(BF16), k.astype(BF16), v.astype(BF16)

        dmat = jnp.where(tril, bc - br + ir, -jnp.inf)
        inter = bc + mst
        m_row = jnp.maximum(inter, jnp.max(dmat, axis=-1, keepdims=True))
        w_inter = jnp.exp(inter - m_row)
        qk = lax.dot_general(qb16, kb16, (((1,), (1,)), ((), ())), preferred_element_type=F32)
        qk = qk * jnp.exp(dmat - m_row)
        num = (w_inter * jnp.dot(qb16, c_s[h].astype(BF16), preferred_element_type=F32)
               + jnp.dot(qk.astype(BF16), vb16, preferred_element_type=F32))
        den = (w_inter * jnp.sum(q * n_s[h, 0:1, :], axis=-1, keepdims=True)
               + jnp.sum(qk, axis=-1, keepdims=True))
        hh = num / jnp.maximum(jnp.abs(den), jnp.exp(-m_row))

        b_last = bc[lc - 1:lc, :]
        g_col = b_last - bc + ic
        g_row = b_last - br + ir
        m_new = jnp.maximum(b_last + mst, jnp.max(g_row, axis=-1, keepdims=True))
        decay = jnp.exp(b_last + mst - m_new)
        kw = jnp.exp(g_col - m_new) * k
        c_s[h] = decay * c_s[h] + lax.dot_general(kw.astype(BF16), vb16, (((0,), (0,)), ((), ())),
                                                   preferred_element_type=F32)
        n_s[h, 0:1, :] = decay * n_s[h, 0:1, :] + jnp.sum(kw, axis=0, keepdims=True)
        m_s[h] = jnp.broadcast_to(m_new, (SUBLANE, LANE))

        hn = _rms(hh, on_ref[:, h * dv:(h + 1) * dv])
        o_ref[:, h * dv:(h + 1) * dv] = (hn * _sigmoid(mo_ref[:, h * dv:(h + 1) * dv])).astype(o_ref.dtype)


def _mlstm(p3, conv_w, conv_b, gate_b, out_gain):
    b, s, _ = p3.shape
    lc = MLSTM_CHUNK
    assert s % lc == 0 and lc % LANE == 0
    return pl.pallas_call(
        _mlstm_kernel,
        grid=(b, s // lc),
        in_specs=[
            pl.BlockSpec((None, lc, MQK), lambda i, j: (i, j, OFF_MQK // MQK)),
            pl.BlockSpec((None, lc, MV), lambda i, j: (i, j, OFF_MV // MV)),
            pl.BlockSpec((None, lc, MV), lambda i, j: (i, j, OFF_MO // MV)),
            pl.BlockSpec((None, lc, LANE), lambda i, j: (i, j, OFF_MG // LANE)),
            pl.BlockSpec((CONV_WIDTH, MQK), lambda i, j: (0, 0)),
            pl.BlockSpec((1, MQK), lambda i, j: (0, 0)),
            pl.BlockSpec((1, LANE), lambda i, j: (0, 0)),
            pl.BlockSpec((1, MV), lambda i, j: (0, 0)),
        ],
        out_specs=pl.BlockSpec((None, lc, MV), lambda i, j: (i, j, 0)),
        out_shape=jax.ShapeDtypeStruct((b, s, MV), BF16),
        scratch_shapes=[
            pltpu.VMEM((M_HEADS, M_QK_DIM, M_V_DIM), F32),
            pltpu.VMEM((M_HEADS, SUBLANE, M_QK_DIM), F32),
            pltpu.VMEM((M_HEADS, SUBLANE, LANE), F32),
            pltpu.VMEM((SUBLANE + lc, MQK), F32),
        ],
        compiler_params=_params(2),
        name="mlstm",
    )(p3, p3, p3, p3, conv_w, conv_b, gate_b, out_gain)


def _mergeout_kernel(oa_ref, hm_ref, ga_ref, gm_ref, h_ref, wa_ref, wm_ref, wo_ref, o_ref, mg_s, *, tn):
    d = o_ref.shape[1]
    oa = oa_ref[...]
    hm = hm_ref[...]
    for n in range(d // tn):
        sl = slice(n * tn, (n + 1) * tn)
        ya = jnp.dot(oa, wa_ref[:, sl], preferred_element_type=F32)
        ym = jnp.dot(hm, wm_ref[:, sl], preferred_element_type=F32)
        mg_s[:, sl] = (_sigmoid(ga_ref[:, sl]) * ya + _sigmoid(gm_ref[:, sl]) * ym).astype(BF16)
    mg = mg_s[...]
    for n in range(d // tn):
        sl = slice(n * tn, (n + 1) * tn)
        o_ref[:, sl] = h_ref[:, sl] + jnp.dot(mg, wo_ref[:, sl], preferred_element_type=F32)


def _mergeout(oa, hm, p2, h, wa, wm, wo, l):
    m, d = h.shape
    tm = _tile(m, 256)
    tn = _tile(d, 512)
    assert OFF_GA % d == 0
    ga0 = OFF_GA // d
    resident = functools.partial(pl.BlockSpec, pipeline_mode=pl.Buffered(1))
    return pl.pallas_call(
        functools.partial(_mergeout_kernel, tn=tn),
        grid=(m // tm,),
        in_specs=[
            pl.BlockSpec((tm, HR), lambda i: (i, 0)),
            pl.BlockSpec((tm, MV), lambda i: (i, 0)),
            pl.BlockSpec((tm, d), lambda i: (i, ga0)),
            pl.BlockSpec((tm, d), lambda i: (i, ga0 + 1)),
            pl.BlockSpec((tm, d), lambda i: (i, 0)),
            resident((None, HR, d), lambda i: (l, 0, 0)),
            resident((None, MV, d), lambda i: (l, 0, 0)),
            resident((None, d, d), lambda i: (l, 0, 0)),
        ],
        out_specs=pl.BlockSpec((tm, d), lambda i: (i, 0)),
        out_shape=jax.ShapeDtypeStruct((m, d), F32),
        scratch_shapes=[pltpu.VMEM((tm, d), BF16)],
        compiler_params=_params(1),
        name="mergeout",
    )(oa, hm, p2, p2, h, wa, wm, wo)


def _bias_tiles(rel_bias):
    tq = Q_BLOCK
    i = jnp.arange(tq)[:, None]
    j = jnp.arange(tq)[None, :]

    def bucket(rel):
        n = jnp.maximum(rel, 0)
        max_exact = REL_BUCKETS // 2
        n_large = jnp.maximum(n, max_exact).astype(F32)
        large = max_exact + (jnp.log(n_large / max_exact) / math.log(REL_MAX_DIST / max_exact)
                             * (REL_BUCKETS - max_exact)).astype(jnp.int32)
        large = jnp.minimum(large, REL_BUCKETS - 1)
        return jnp.where(n < max_exact, n, large)

    onehot = jax.nn.one_hot(jnp.stack([bucket(i - j + d * tq) for d in range(3)]), REL_BUCKETS, dtype=F32)
    return jnp.einsum("dijk,kh->dhij", onehot, rel_bias.astype(F32) * LOG2E, precision=lax.Precision.HIGHEST)


def _w_in_kernel(w_ref, o_ref, *, segs):
    for dst, src, width in segs:
        if src is None:
            o_ref[0, dst:dst + width, :] = jnp.zeros((width, o_ref.shape[2]), o_ref.dtype)
        else:
            o_ref[0, dst:dst + width, :] = w_ref[0, src:src + width, :].astype(o_ref.dtype)


def _prep_w_in(w_in, d):
    depth, dm, din = w_in.shape
    widths = (HR, KV_RANK, IQW, IDX_DIM, IDX_HEADS, MQK, MV, MV, M_HEADS, M_HEADS, d, d)
    offs = [0]
    for wdt in widths:
        offs.append(offs[-1] + wdt)
    assert offs[-1] == din
    aq, ckv, iq, ik, _, mqk, _, _, mi, _, ga, _ = offs[:12]
    pieces = [(aq, HR), (iq, IQW), (ckv, KV_RANK), (ik, IDX_DIM + IDX_HEADS), (None, LANE - IDX_DIM - IDX_HEADS),
              (mi, 2 * M_HEADS), (None, LANE - 2 * M_HEADS), (mqk, MQK + 2 * MV), (ga, 2 * d)]
    segs = []
    dst = 0
    for src, width in pieces:
        segs.append((dst, src, width))
        dst += width
    assert all(o % SUBLANE == 0 and (src is None or src % SUBLANE == 0) for o, src, _ in segs)
    tc = _tile(dm, 256)
    return pl.pallas_call(
        functools.partial(_w_in_kernel, segs=tuple(segs)),
        grid=(depth, dm // tc),
        in_specs=[pl.BlockSpec((1, din, tc), lambda l, i: (l, 0, i))],
        out_specs=pl.BlockSpec((1, dst, tc), lambda l, i: (l, 0, i)),
        out_shape=jax.ShapeDtypeStruct((depth, dst, dm), BF16),
        compiler_params=_params(2),
        name="prep_w_in",
    )(jnp.swapaxes(w_in, 1, 2))


def kernel(x, rel_bias, ffn1_norm, ffn1_w_gu, ffn1_w_down, mix_norm, w_in, q_norm, kv_norm, conv_w, conv_b,
           igate_b, fgate_b, m_out_norm, w_att_out, w_mem_out, w_out, ffn2_norm, ffn2_w_gu, ffn2_w_down):
    b, s, d = x.shape
    depth = w_in.shape[0]
    topk = min(TOPK_MAX, s // 4)
    m = b * s
    h = x.reshape(m, d)
    bias_tiles = _bias_tiles(rel_bias)
    w1gu, w2gu = ffn1_w_gu.astype(BF16), ffn2_w_gu.astype(BF16)
    w1d, w2d = ffn1_w_down, ffn2_w_down
    w_in_r = _prep_w_in(w_in, d)
    w_att = w_att_out.astype(BF16).reshape(depth, HR, d)
    w_mem = w_mem_out.astype(BF16)
    w_o = w_out.astype(BF16)
    gate_b = jnp.concatenate([igate_b, fgate_b, jnp.zeros((depth, LANE - 2 * M_HEADS), F32)], axis=1)
    for l in range(depth):
        h = _ffn(h, ffn1_norm[l][None], w1gu, w1d, l)

        q2, p2 = _inproj(h, mix_norm[l][None], w_in_r, q_norm[l][None], l)
        p3 = p2.reshape(b, s, p2.shape[1])
        o_att = _dsa(q2.reshape(b, s, HR), p3, bias_tiles, kv_norm[l][None], topk)
        hm = _mlstm(p3, conv_w[l], conv_b[l][None], gate_b[l][None], m_out_norm[l].reshape(1, MV))
        h = _mergeout(o_att.reshape(m, HR), hm.reshape(m, MV), p2, h, w_att, w_mem, w_o, l)

        h = _ffn(h, ffn2_norm[l][None], w2gu, w2d, l)
    return h.reshape(b, s, d)
```
